```python
import math
import jax, jax.numpy as jnp
from jax import lax
import numpy as np

D_MODEL = 2048
BATCH = 4
SEQ = 2048
DEPTH = 1

MEM_TOKENS = 256
ROPE_THETA = 10000.0
Q_BLOCK = 128
EPS = 1e-6
MLA_HEADS = 8
MLA_Q_RANK = 512
MLA_KV_RANK = 512
MLA_NOPE_DIM = 128
MLA_ROPE_DIM = 64
MLA_V_DIM = 128
DIFF_HEADS = 8
DIFF_HEAD_DIM = 64
DIFF_V_DIM = 2 * DIFF_HEAD_DIM
N_BRANCHES = 2
XATTN_HEADS = 4
XATTN_HEAD_DIM = 128
FFN_DIM = 5632
CONV_WIDTH = 3
IN_DIM = (MLA_Q_RANK + MLA_KV_RANK + MLA_ROPE_DIM + 2 * DIFF_HEADS * 2 * DIFF_HEAD_DIM
          + DIFF_HEADS * DIFF_V_DIM + N_BRANCHES * D_MODEL)

kernel_name = 'hybrid_mla_diffattn_convffn_encoder'


def _in_split_points():
    sizes = (MLA_Q_RANK, MLA_KV_RANK, MLA_ROPE_DIM, DIFF_HEADS * 2 * DIFF_HEAD_DIM,
             DIFF_HEADS * 2 * DIFF_HEAD_DIM, DIFF_HEADS * DIFF_V_DIM, N_BRANCHES * D_MODEL)
    points, acc = [], 0
    for s in sizes[:-1]:
        acc += s
        points.append(acc)
    return points


def rms_norm(x, g):
    xf = x.astype(jnp.float32)
    y = xf * lax.rsqrt(jnp.mean(xf * xf, axis=-1, keepdims=True) + EPS)
    return (y * g.astype(jnp.float32)).astype(x.dtype)


def rope_tables(positions, dim, dtype):
    inv = ROPE_THETA ** (-jnp.arange(0, dim, 2, dtype=jnp.float32) / dim)
    ang = positions.astype(jnp.float32)[..., None] * inv
    return jnp.cos(ang)[:, :, None, :].astype(dtype), jnp.sin(ang)[:, :, None, :].astype(dtype)


def apply_rope(x, cos, sin):
    x1, x2 = jnp.split(x, 2, axis=-1)
    return jnp.concatenate([x1 * cos - x2 * sin, x2 * cos + x1 * sin], axis=-1)


def query_blocks(q):
    b, s = q.shape[:2]
    q = q.reshape((b, s // Q_BLOCK, Q_BLOCK) + q.shape[2:])
    return jnp.moveaxis(q, 1, 0)


def merge_blocks(o):
    o = jnp.moveaxis(o, 0, 1)
    return o.reshape((o.shape[0], o.shape[1] * o.shape[2]) + o.shape[3:])


def mla_attention(q_nope, q_pe, k_nope, k_pe, v):
    scale = (MLA_NOPE_DIM + MLA_ROPE_DIM) ** -0.5

    def block(qs):
        qn, qp = qs
        s = (jnp.einsum('bqhd,bkhd->bhqk', qn, k_nope)
             + jnp.einsum('bqhd,bkd->bhqk', qp, k_pe))
        p = jax.nn.softmax(s.astype(jnp.float32) * scale, axis=-1).astype(v.dtype)
        return jnp.einsum('bhqk,bkhd->bqhd', p, v)

    return merge_blocks(lax.map(block, (query_blocks(q_nope), query_blocks(q_pe))))


def diff_attention(q, k, v, lam):
    scale = DIFF_HEAD_DIM ** -0.5

    def block(qb):
        s = jnp.einsum('bqhcd,bkhcd->bhcqk', qb, k)
        p = jax.nn.softmax(s.astype(jnp.float32) * scale, axis=-1)
        a = (p[:, :, 0] - lam * p[:, :, 1]).astype(v.dtype)
        return jnp.einsum('bhqk,bkhd->bqhd', a, v)

    return merge_blocks(lax.map(block, query_blocks(q)))


def hybrid_mixer(h, cos_m, sin_m, cos_d, sin_d, lam_init, w_in, g_q_norm, w_uq, g_kv_norm, w_ukv,
                 w_o_mla, lambda_q1, lambda_k1, lambda_q2, lambda_k2, g_diff_sub, w_o_diff, w_out):
    b, s = h.shape[:2]
    z = h @ w_in
    c_q, c_kv, k_pe, dq, dk, dv, gate_logits = jnp.split(z, _in_split_points(), axis=-1)

    q = (rms_norm(c_q, g_q_norm) @ w_uq).reshape(b, s, MLA_HEADS, MLA_NOPE_DIM + MLA_ROPE_DIM)
    q_nope = q[..., :MLA_NOPE_DIM]
    q_pe = apply_rope(q[..., MLA_NOPE_DIM:], cos_m, sin_m)
    kv = (rms_norm(c_kv, g_kv_norm) @ w_ukv).reshape(b, s, MLA_HEADS, MLA_NOPE_DIM + MLA_V_DIM)
    k_nope, v_m = kv[..., :MLA_NOPE_DIM], kv[..., MLA_NOPE_DIM:]
    k_pe = apply_rope(k_pe[:, :, None, :], cos_m, sin_m)[:, :, 0]
    y_a = mla_attention(q_nope, q_pe, k_nope, k_pe, v_m).reshape(b, s, MLA_HEADS * MLA_V_DIM) @ w_o_mla

    dq = apply_rope(dq.reshape(b, s, DIFF_HEADS * 2, DIFF_HEAD_DIM), cos_d, sin_d)
    dk = apply_rope(dk.reshape(b, s, DIFF_HEADS * 2, DIFF_HEAD_DIM), cos_d, sin_d)
    dq = dq.reshape(b, s, DIFF_HEADS, 2, DIFF_HEAD_DIM)
    dk = dk.reshape(b, s, DIFF_HEADS, 2, DIFF_HEAD_DIM)
    dv = dv.reshape(b, s, DIFF_HEADS, DIFF_V_DIM)
    lam = (jnp.exp(jnp.sum(lambda_q1.astype(jnp.float32) * lambda_k1.astype(jnp.float32)))
           - jnp.exp(jnp.sum(lambda_q2.astype(jnp.float32) * lambda_k2.astype(jnp.float32)))
           + lam_init)
    o = diff_attention(dq, dk, dv, lam)
    o = rms_norm(o, g_diff_sub) * (1.0 - lam_init)
    y_b = o.reshape(b, s, DIFF_HEADS * DIFF_V_DIM) @ w_o_diff

    g_a, g_b = jnp.split(jax.nn.sigmoid(gate_logits), N_BRANCHES, axis=-1)
    return (g_a * y_a + g_b * y_b) @ w_out


def memory_cross_attention(h, mem_n, w_q, w_kv, w_o):
    b, s = h.shape[:2]
    m = mem_n.shape[1]
    q = (h @ w_q).reshape(b, s, XATTN_HEADS, XATTN_HEAD_DIM)
    kv = (mem_n @ w_kv).reshape(b, m, XATTN_HEADS, 2 * XATTN_HEAD_DIM)
    k, v = kv[..., :XATTN_HEAD_DIM], kv[..., XATTN_HEAD_DIM:]
    sc = jnp.einsum('bqhd,bkhd->bhqk', q, k).astype(jnp.float32) * (XATTN_HEAD_DIM ** -0.5)
    p = jax.nn.softmax(sc, axis=-1).astype(v.dtype)
    o = jnp.einsum('bhqk,bkhd->bqhd', p, v).reshape(b, s, XATTN_HEADS * XATTN_HEAD_DIM)
    return o @ w_o


def depthwise_conv(u, w, bias):
    y = lax.conv_general_dilated(
        u, w[:, None, :].astype(u.dtype), window_strides=(1,),
        padding=((CONV_WIDTH // 2, CONV_WIDTH // 2),),
        dimension_numbers=('NWC', 'WIO', 'NWC'), feature_group_count=u.shape[-1])
    return y + bias.astype(u.dtype)


def conv_ffn(h, w_up, conv_w, conv_b, w_down):
    u = depthwise_conv(h @ w_up, conv_w, conv_b)
    gate, val = jnp.split(u, 2, axis=-1)
    return (jax.nn.silu(gate) * val) @ w_down


def setup_inputs(seed: int = 0) -> dict:
    key = jax.random.key(seed)
    ks = jax.random.split(key, 32)
    f32 = jnp.float32

    def dense(k, fan_in, fan_out):
        return jax.random.normal(k, (DEPTH, fan_in, fan_out), f32) * fan_in ** -0.5

    def gain(k, dim):
        return 1.0 + 0.02 * jax.random.normal(k, (DEPTH, dim), f32)

    x = jax.random.normal(ks[0], (BATCH, SEQ, D_MODEL), f32)
    mem = jax.random.normal(ks[1], (BATCH, MEM_TOKENS, D_MODEL), f32)
    positions = (jnp.arange(SEQ, dtype=jnp.int32)[None, :]
                 + jax.random.randint(ks[2], (BATCH, 1), 0, SEQ, dtype=jnp.int32))
    return {
        'x': x,
        'mem': mem,
        'positions': positions,
        'g_mix_norm': gain(ks[3], D_MODEL),
        'w_in': dense(ks[4], D_MODEL, IN_DIM),
        'g_q_norm': gain(ks[5], MLA_Q_RANK),
        'w_uq': dense(ks[6], MLA_Q_RANK, MLA_HEADS * (MLA_NOPE_DIM + MLA_ROPE_DIM)),
        'g_kv_norm': gain(ks[7], MLA_KV_RANK),
        'w_ukv': dense(ks[8], MLA_KV_RANK, MLA_HEADS * (MLA_NOPE_DIM + MLA_V_DIM)),
        'w_o_mla': dense(ks[9], MLA_HEADS * MLA_V_DIM, D_MODEL),
        'lambda_q1': 0.1 * jax.random.normal(ks[10], (DEPTH, DIFF_HEAD_DIM), f32),
        'lambda_k1': 0.1 * jax.random.normal(ks[11], (DEPTH, DIFF_HEAD_DIM), f32),
        'lambda_q2': 0.1 * jax.random.normal(ks[12], (DEPTH, DIFF_HEAD_DIM), f32),
        'lambda_k2': 0.1 * jax.random.normal(ks[13], (DEPTH, DIFF_HEAD_DIM), f32),
        'g_diff_sub': gain(ks[14], DIFF_V_DIM),
        'w_o_diff': dense(ks[15], DIFF_HEADS * DIFF_V_DIM, D_MODEL),
        'w_out': dense(ks[16], D_MODEL, D_MODEL),
        'g_cross_norm': gain(ks[17], D_MODEL),
        'g_mem_norm': gain(ks[18], D_MODEL),
        'w_cross_q': dense(ks[19], D_MODEL, XATTN_HEADS * XATTN_HEAD_DIM),
        'w_cross_kv': dense(ks[20], D_MODEL, 2 * XATTN_HEADS * XATTN_HEAD_DIM),
        'w_cross_o': dense(ks[21], XATTN_HEADS * XATTN_HEAD_DIM, D_MODEL),
        'g_ffn_norm': gain(ks[22], D_MODEL),
        'w_up': dense(ks[23], D_MODEL, 2 * FFN_DIM),
        'conv_w': jax.random.normal(ks[24], (DEPTH, CONV_WIDTH, 2 * FFN_DIM), f32) * CONV_WIDTH ** -0.5,
        'conv_b': 0.01 * jax.random.normal(ks[25], (DEPTH, 2 * FFN_DIM), f32),
        'w_down': dense(ks[26], FFN_DIM, D_MODEL),
        'g_final': 1.0 + 0.02 * jax.random.normal(ks[27], (D_MODEL,), f32),
    }


def reference(x, mem, positions, g_mix_norm, w_in, g_q_norm, w_uq, g_kv_norm, w_ukv, w_o_mla,
              lambda_q1, lambda_k1, lambda_q2, lambda_k2, g_diff_sub, w_o_diff, w_out,
              g_cross_norm, g_mem_norm, w_cross_q, w_cross_kv, w_cross_o,
              g_ffn_norm, w_up, conv_w, conv_b, w_down, g_final):
    cos_m, sin_m = rope_tables(positions, MLA_ROPE_DIM, x.dtype)
    cos_d, sin_d = rope_tables(positions, DIFF_HEAD_DIM, x.dtype)
    for l in range(DEPTH):
        lam_init = 0.8 - 0.6 * math.exp(-0.3 * l)
        x = x + hybrid_mixer(
            rms_norm(x, g_mix_norm[l]), cos_m, sin_m, cos_d, sin_d, lam_init,
            w_in[l], g_q_norm[l], w_uq[l], g_kv_norm[l], w_ukv[l], w_o_mla[l],
            lambda_q1[l], lambda_k1[l], lambda_q2[l], lambda_k2[l], g_diff_sub[l], w_o_diff[l], w_out[l])
        x = x + memory_cross_attention(
            rms_norm(x, g_cross_norm[l]), rms_norm(mem, g_mem_norm[l]),
            w_cross_q[l], w_cross_kv[l], w_cross_o[l])
        x = x + conv_ffn(rms_norm(x, g_ffn_norm[l]), w_up[l], conv_w[l], conv_b[l], w_down[l])
    return rms_norm(x, g_final)
```

```python
import functools
import math

import jax
import jax.numpy as jnp
from jax import lax
from jax.experimental import pallas as pl
from jax.experimental.pallas import tpu as pltpu

F32 = jnp.float32
BF16 = jnp.bfloat16

EPS = 1e-6
ROPE_THETA = 10000.0
MLA_HEADS = 8
MLA_NOPE_DIM = 128
MLA_ROPE_DIM = 64
MLA_V_DIM = 128
MLA_QK_PAD = 256
DIFF_HEADS = 8
DIFF_HEAD_DIM = 64
DIFF_V_DIM = 2 * DIFF_HEAD_DIM
XATTN_HEADS = 4
XATTN_HEAD_DIM = 128
LANES = 128
BF16_SUBLANES = 16
VMEM_LIMIT_BYTES = 56 * 1024 * 1024

_NT = (((1,), (1,)), ((), ()))


def _params(*semantics):
    return pltpu.CompilerParams(dimension_semantics=semantics, vmem_limit_bytes=VMEM_LIMIT_BYTES)


def _rms(x, g):
    ms = jnp.mean(x * x, axis=-1, keepdims=True)
    return x * lax.rsqrt(ms + EPS) * g


def _sigmoid(x):
    return 1.0 / (1.0 + jnp.exp(-x))


def _rope128(x, cos_t, sin_t):
    lane = lax.broadcasted_iota(jnp.int32, x.shape, 1)
    first_half = (lane & (MLA_ROPE_DIM // 2)) == 0
    partner = jnp.where(first_half, pltpu.roll(x, LANES - 32, 1), pltpu.roll(x, 32, 1))
    return x * cos_t + partner * sin_t


def _softmax_parts(s):
    m = jnp.max(s, axis=-1, keepdims=True)
    e = jnp.exp(s - m)
    return e, jnp.sum(e, axis=-1, keepdims=True)


def _mla_proj_kernel(x_ref, g_ref, w1_ref, gq_ref, gkv_ref, wq_ref, wkv_ref, cos_ref, sin_ref,
                     q_ref, k_ref, v_ref, *, q_rank, kv_rank, q_scale):
    xn = _rms(x_ref[...], g_ref[...]).astype(BF16)
    z = jnp.dot(xn, w1_ref[...], preferred_element_type=F32)
    cqn = _rms(z[:, :q_rank], gq_ref[...]).astype(BF16)
    ckvn = _rms(z[:, q_rank:q_rank + kv_rank], gkv_ref[...]).astype(BF16)
    cos_t, sin_t = cos_ref[...], sin_ref[...]
    kpe = _rope128(z[:, q_rank + kv_rank:], cos_t, sin_t).astype(BF16)
    q = jnp.dot(cqn, wq_ref[...], preferred_element_type=F32) * q_scale
    kv = jnp.dot(ckvn, wkv_ref[...], preferred_element_type=F32)
    for h in range(MLA_HEADS):
        lo, mid, hi = h * MLA_QK_PAD, h * MLA_QK_PAD + MLA_NOPE_DIM, (h + 1) * MLA_QK_PAD
        q_ref[:, lo:mid] = q[:, lo:mid].astype(BF16)
        q_ref[:, mid:hi] = _rope128(q[:, mid:hi], cos_t, sin_t).astype(BF16)
        k_ref[:, lo:mid] = kv[:, h * MLA_NOPE_DIM:(h + 1) * MLA_NOPE_DIM].astype(BF16)
        k_ref[:, mid:hi] = kpe
    v_ref[...] = kv[:, MLA_HEADS * MLA_NOPE_DIM:].astype(BF16)


def _mla_proj(x, g, w1, gq, gkv, wq, wkv, cos_t, sin_t, *, tm):
    t, d = x.shape
    q_rank, kv_rank = gq.shape[1], gkv.shape[1]
    full = lambda a: pl.BlockSpec(a.shape, lambda i: (0, 0))
    rows = lambda width: pl.BlockSpec((tm, width), lambda i: (i, 0))
    qk_w, v_w = MLA_HEADS * MLA_QK_PAD, MLA_HEADS * MLA_V_DIM
    kern = functools.partial(_mla_proj_kernel, q_rank=q_rank, kv_rank=kv_rank,
                             q_scale=(MLA_NOPE_DIM + MLA_ROPE_DIM) ** -0.5)
    return pl.pallas_call(
        kern,
        grid=(t // tm,),
        in_specs=[rows(d), full(g), full(w1), full(gq), full(gkv), full(wq), full(wkv),
                  rows(LANES), rows(LANES)],
        out_specs=[rows(qk_w), rows(qk_w), rows(v_w)],
        out_shape=[jax.ShapeDtypeStruct((t, qk_w), BF16), jax.ShapeDtypeStruct((t, qk_w), BF16),
                   jax.ShapeDtypeStruct((t, v_w), BF16)],
        compiler_params=_params("parallel"),
        name="mla_proj",
    )(x, g, w1, gq, gkv, wq, wkv, cos_t, sin_t)


def _in_proj_kernel(x_ref, g_ref, w_ref, cos_ref, sin_ref, o_ref, xn_ref, *, gate_blocks, dq_scale):
    j = pl.program_id(1)

    @pl.when(j == 0)
    def _():
        xn_ref[...] = _rms(x_ref[...], g_ref[...]).astype(BF16)

    z = jnp.dot(xn_ref[...], w_ref[...], preferred_element_type=F32)

    def rope_out(scale):
        cos_t, sin_t = cos_ref[...], sin_ref[...]
        for c in range(z.shape[1] // LANES):
            zc = z[:, c * LANES:(c + 1) * LANES]
            if scale != 1.0:
                zc = zc * scale
            o_ref[:, c * LANES:(c + 1) * LANES] = _rope128(zc, cos_t, sin_t).astype(BF16)

    @pl.when(j < gate_blocks)
    def _():
        o_ref[...] = _sigmoid(z).astype(BF16)

    @pl.when(j == gate_blocks)
    def _():
        rope_out(dq_scale)

    @pl.when(j == gate_blocks + 1)
    def _():
        rope_out(1.0)

    @pl.when(j == gate_blocks + 2)
    def _():
        o_ref[...] = z.astype(BF16)


def _in_proj(x, g, w, cos_t, sin_t, *, tm, tn, gate_blocks):
    t, d = x.shape
    n = w.shape[1]
    kern = functools.partial(_in_proj_kernel, gate_blocks=gate_blocks, dq_scale=DIFF_HEAD_DIM ** -0.5)
    return pl.pallas_call(
        kern,
        grid=(t // tm, n // tn),
        in_specs=[pl.BlockSpec((tm, d), lambda i, j: (i, 0)),
                  pl.BlockSpec((1, d), lambda i, j: (0, 0)),
                  pl.BlockSpec((d, tn), lambda i, j: (0, j)),
                  pl.BlockSpec((tm, LANES), lambda i, j: (i, 0)),
                  pl.BlockSpec((tm, LANES), lambda i, j: (i, 0))],
        out_specs=pl.BlockSpec((tm, tn), lambda i, j: (i, j)),
        out_shape=jax.ShapeDtypeStruct((t, n), BF16),
        scratch_shapes=[pltpu.VMEM((tm, d), BF16)],
        compiler_params=_params("parallel", "arbitrary"),
        name="in_proj",
    )(x, g, w, cos_t, sin_t)


def _mla_attn_kernel(q_ref, k_ref, v_ref, o_ref, *, tq):
    def body(r, carry):
        rows = pl.ds(pl.multiple_of(r * tq, tq), tq)
        s = lax.dot_general(q_ref[rows, :], k_ref[...], _NT, preferred_element_type=F32)
        e, l = _softmax_parts(s)
        o = jnp.dot(e.astype(BF16), v_ref[...], preferred_element_type=F32) * (1.0 / l)
        o_ref[rows, :] = o.astype(BF16)
        return carry

    lax.fori_loop(0, q_ref.shape[0] // tq, body, 0)


def _mla_attn(q, k, v, *, batch, tq):
    t = q.shape[0]
    s = t // batch
    return pl.pallas_call(
        functools.partial(_mla_attn_kernel, tq=tq),
        grid=(batch, MLA_HEADS),
        in_specs=[pl.BlockSpec((s, MLA_QK_PAD), lambda b, h: (b, h)),
                  pl.BlockSpec((s, MLA_QK_PAD), lambda b, h: (b, h)),
                  pl.BlockSpec((s, MLA_V_DIM), lambda b, h: (b, h))],
        out_specs=pl.BlockSpec((s, MLA_V_DIM), lambda b, h: (b, h)),
        out_shape=jax.ShapeDtypeStruct((t, MLA_HEADS * MLA_V_DIM), BF16),
        compiler_params=_params("parallel", "parallel"),
        name="mla_attn",
    )(q, k, v)


def _diff_attn_kernel(lq1_ref, lk1_ref, lq2_ref, lk2_ref, gsub_ref, q_ref, k_ref, v_ref, o_ref,
                      *, tq, lam_init):
    lam = (jnp.exp(jnp.sum(lq1_ref[...] * lk1_ref[...], axis=-1, keepdims=True))
           - jnp.exp(jnp.sum(lq2_ref[...] * lk2_ref[...], axis=-1, keepdims=True)) + lam_init)
    gain = gsub_ref[...] * (1.0 - lam_init)
    lane = lax.broadcasted_iota(jnp.int32, (tq, 2 * DIFF_HEAD_DIM), 1)
    map0 = lane < DIFF_HEAD_DIM

    def body(r, carry):
        rows = pl.ds(pl.multiple_of(r * tq, tq), tq)
        q = q_ref[rows, :]
        zero = jnp.zeros_like(q)
        q01 = jnp.concatenate([jnp.where(map0, q, zero), jnp.where(map0, zero, q)], axis=0)
        s = lax.dot_general(q01, k_ref[...], _NT, preferred_element_type=F32)
        e, l = _softmax_parts(s)
        inv = 1.0 / l
        a = e[:tq] * inv[:tq] - e[tq:] * (lam * inv[tq:])
        o = jnp.dot(a.astype(BF16), v_ref[...], preferred_element_type=F32)
        o_ref[rows, :] = _rms(o, gain).astype(BF16)
        return carry

    lax.fori_loop(0, q_ref.shape[0] // tq, body, 0)


def _diff_attn(lq1, lk1, lq2, lk2, gsub, z, *, batch, tq, q_col, k_col, v_col, lam_init):
    t = z.shape[0]
    s = t // batch
    hw = 2 * DIFF_HEAD_DIM
    small = lambda a: pl.BlockSpec(a.shape, lambda b, h: (0, 0))
    head = lambda col: pl.BlockSpec((s, hw), lambda b, h: (b, col + h))
    return pl.pallas_call(
        functools.partial(_diff_attn_kernel, tq=tq, lam_init=lam_init),
        grid=(batch, DIFF_HEADS),
        in_specs=[small(lq1), small(lk1), small(lq2), small(lk2), small(gsub),
                  head(q_col), head(k_col), head(v_col)],
        out_specs=pl.BlockSpec((s, DIFF_V_DIM), lambda b, h: (b, h)),
        out_shape=jax.ShapeDtypeStruct((t, DIFF_HEADS * DIFF_V_DIM), BF16),
        compiler_params=_params("parallel", "parallel"),
        name="diff_attn",
    )(lq1, lk1, lq2, lk2, gsub, z, z, z)


def _mem_kv_kernel(m_ref, g_ref, w_ref, o_ref):
    mn = _rms(m_ref[...], g_ref[...]).astype(BF16)
    o_ref[...] = jnp.dot(mn, w_ref[...], preferred_element_type=F32).astype(BF16)


def _mem_kv(mem, g, w, *, tm):
    t, d = mem.shape
    n = w.shape[1]
    return pl.pallas_call(
        _mem_kv_kernel,
        grid=(t // tm,),
        in_specs=[pl.BlockSpec((tm, d), lambda i: (i, 0)),
                  pl.BlockSpec((1, d), lambda i: (0, 0)),
                  pl.BlockSpec((d, n), lambda i: (0, 0))],
        out_specs=pl.BlockSpec((tm, n), lambda i: (i, 0)),
        out_shape=jax.ShapeDtypeStruct((t, n), BF16),
        compiler_params=_params("parallel"),
        name="mem_kv",
    )(mem, g, w)


def _merge_cross_kernel(x_ref, oa_ref, ob_ref, gate_ref, woa_ref, wob_ref, wout_ref,
                        gc_ref, wcq_ref, mkv_ref, wco_ref, o_ref, *, q_scale):
    d = x_ref.shape[1]
    ya = jnp.dot(oa_ref[...], woa_ref[...], preferred_element_type=F32)
    yb = jnp.dot(ob_ref[...], wob_ref[...], preferred_element_type=F32)
    merged = gate_ref[:, :d].astype(F32) * ya + gate_ref[:, d:].astype(F32) * yb
    x1 = x_ref[...] + jnp.dot(merged.astype(BF16), wout_ref[...], preferred_element_type=F32)

    hn = _rms(x1, gc_ref[...]).astype(BF16)
    qc = (jnp.dot(hn, wcq_ref[...], preferred_element_type=F32) * q_scale).astype(BF16)
    hd = XATTN_HEAD_DIM
    heads = []
    for h in range(XATTN_HEADS):
        k_h = mkv_ref[:, h * hd:(h + 1) * hd]
        v_h = mkv_ref[:, (XATTN_HEADS + h) * hd:(XATTN_HEADS + h + 1) * hd]
        s = lax.dot_general(qc[:, h * hd:(h + 1) * hd], k_h, _NT, preferred_element_type=F32)
        e, l = _softmax_parts(s)
        o_h = jnp.dot(e.astype(BF16), v_h, preferred_element_type=F32) * (1.0 / l)
        heads.append(o_h.astype(BF16))
    oc = jnp.concatenate(heads, axis=1)
    o_ref[...] = x1 + jnp.dot(oc, wco_ref[...], preferred_element_type=F32)


def _merge_cross(x, oa, ob, z, woa, wob, wout, gc, wcq, mkv, wco, *, tm, batch):
    t, d = x.shape
    blocks_per_seq = t // batch // tm
    m = mkv.shape[0] // batch
    const = lambda a: pl.BlockSpec(a.shape, lambda i: (0, 0), pipeline_mode=pl.Buffered(1))
    rows = lambda width: pl.BlockSpec((tm, width), lambda i: (i, 0))
    return pl.pallas_call(
        functools.partial(_merge_cross_kernel, q_scale=XATTN_HEAD_DIM ** -0.5),
        grid=(t // tm,),
        in_specs=[rows(d), rows(oa.shape[1]), rows(ob.shape[1]), rows(2 * d),
                  const(woa), const(wob), const(wout), const(gc), const(wcq),
                  pl.BlockSpec((m, mkv.shape[1]), lambda i: (i // blocks_per_seq, 0)),
                  const(wco)],
        out_specs=rows(d),
        out_shape=jax.ShapeDtypeStruct((t, d), F32),
        compiler_params=_params("parallel"),
        name="merge_cross",
    )(x, oa, ob, z, woa, wob, wout, gc, wcq, mkv, wco)


def _conv_ffn_kernel(xp_ref, x_ref, xn_ref, g_ref, wg_ref, wv_ref, cwg_ref, cwv_ref, cbg_ref, cbv_ref,
                     wd_ref, gfin_ref, o_ref, h_ref, ug_ref, uv_ref, acc_ref,
                     *, tm, halo, blocks_per_seq, final_norm):
    i, f = pl.program_id(0), pl.program_id(1)

    @pl.when(f == 0)
    def _():
        g = g_ref[...]
        pos = i % blocks_per_seq
        keep_prev = jnp.where(pos == 0, 0.0, 1.0)
        keep_next = jnp.where(pos == blocks_per_seq - 1, 0.0, 1.0)
        h_ref[0:halo, :] = (_rms(xp_ref[...], g) * keep_prev).astype(BF16)
        h_ref[halo:halo + tm, :] = _rms(x_ref[...], g).astype(BF16)
        h_ref[halo + tm:, :] = (_rms(xn_ref[...], g) * keep_next).astype(BF16)
        acc_ref[...] = jnp.zeros_like(acc_ref)

    h = h_ref[...]
    ug_ref[...] = jnp.dot(h, wg_ref[...], preferred_element_type=F32)
    uv_ref[...] = jnp.dot(h, wv_ref[...], preferred_element_type=F32)

    def conv(u_ref, cw_ref, cb_ref):
        cw = cw_ref[...]
        return (u_ref[halo - 1:halo - 1 + tm, :] * cw[0:1, :] + u_ref[halo:halo + tm, :] * cw[1:2, :]
                + u_ref[halo + 1:halo + 1 + tm, :] * cw[2:3, :] + cb_ref[...])

    yg = conv(ug_ref, cwg_ref, cbg_ref)
    yv = conv(uv_ref, cwv_ref, cbv_ref)
    act = (yg * _sigmoid(yg) * yv).astype(BF16)
    acc_ref[...] += jnp.dot(act, wd_ref[...], preferred_element_type=F32)

    @pl.when(f == pl.num_programs(1) - 1)
    def _():
        y = x_ref[...] + acc_ref[...]
        o_ref[...] = _rms(y, gfin_ref[...]) if final_norm else y


def _conv_ffn(x, g, w_up, conv_w, conv_b, w_down, g_final, *, tm, tf, batch, final_norm):
    t, d = x.shape
    ffn = w_down.shape[0]
    nf = ffn // tf
    halo = BF16_SUBLANES
    blocks_per_seq = t // batch // tm
    hb = tm // halo
    last_halo_block = t // halo - 1
    kern = functools.partial(_conv_ffn_kernel, tm=tm, halo=halo, blocks_per_seq=blocks_per_seq,
                             final_norm=final_norm)
    return pl.pallas_call(
        kern,
        grid=(t // tm, nf),
        in_specs=[pl.BlockSpec((halo, d), lambda i, f: (jnp.maximum(i * hb - 1, 0), 0)),
                  pl.BlockSpec((tm, d), lambda i, f: (i, 0)),
                  pl.BlockSpec((halo, d), lambda i, f: (jnp.minimum((i + 1) * hb, last_halo_block), 0)),
                  pl.BlockSpec((1, d), lambda i, f: (0, 0)),
                  pl.BlockSpec((d, tf), lambda i, f: (0, f)),
                  pl.BlockSpec((d, tf), lambda i, f: (0, nf + f)),
                  pl.BlockSpec((conv_w.shape[0], tf), lambda i, f: (0, f)),
                  pl.BlockSpec((conv_w.shape[0], tf), lambda i, f: (0, nf + f)),
                  pl.BlockSpec((1, tf), lambda i, f: (0, f)),
                  pl.BlockSpec((1, tf), lambda i, f: (0, nf + f)),
                  pl.BlockSpec((tf, d), lambda i, f: (f, 0)),
                  pl.BlockSpec((1, d), lambda i, f: (0, 0))],
        out_specs=pl.BlockSpec((tm, d), lambda i, f: (i, 0)),
        out_shape=jax.ShapeDtypeStruct((t, d), F32),
        scratch_shapes=[pltpu.VMEM((tm + 2 * halo, d), BF16),
                        pltpu.VMEM((tm + 2 * halo, tf), F32),
                        pltpu.VMEM((tm + 2 * halo, tf), F32),
                        pltpu.VMEM((tm, d), F32)],
        compiler_params=_params("parallel", "arbitrary"),
        name="conv_ffn",
    )(x, x, x, g, w_up, w_up, conv_w, conv_w, conv_b, conv_b, w_down, g_final)


def _rope_tables(positions):
    dim = MLA_ROPE_DIM
    inv = ROPE_THETA ** (-jnp.arange(0, dim, 2, dtype=F32) / dim)
    ang = positions.astype(F32)[..., None] * inv
    cos, sin = jnp.cos(ang), jnp.sin(ang)
    reps = LANES // dim
    cos_t = jnp.concatenate([cos, cos] * reps, axis=-1)
    sin_t = jnp.concatenate([-sin, sin] * reps, axis=-1)
    return cos_t.reshape(-1, LANES), sin_t.reshape(-1, LANES)


def _split_heads_kv(w, heads, dim):
    r = w.shape[0]
    return w.reshape(r, heads, 2, dim).transpose(0, 2, 1, 3).reshape(r, 2 * heads * dim)


def kernel(x, mem, positions, g_mix_norm, w_in, g_q_norm, w_uq, g_kv_norm, w_ukv, w_o_mla, lambda_q1, lambda_k1, lambda_q2, lambda_k2, g_diff_sub, w_o_diff, w_out, g_cross_norm, g_mem_norm, w_cross_q, w_cross_kv, w_cross_o, g_ffn_norm, w_up, conv_w, conv_b, w_down, g_final):
    batch, seq, d = x.shape
    depth = w_in.shape[0]
    t = batch * seq
    assert depth >= 1 and MLA_ROPE_DIM == DIFF_HEAD_DIM and MLA_NOPE_DIM == MLA_V_DIM
    q_rank, kv_rank = g_q_norm.shape[1], g_kv_norm.shape[1]
    dqk = DIFF_HEADS * 2 * DIFF_HEAD_DIM
    dv_w = DIFF_HEADS * DIFF_V_DIM
    c0 = q_rank + kv_rank + MLA_ROPE_DIM
    tn = 1024
    assert dqk == tn and dv_w == tn and (2 * d) % tn == 0

    cos_t, sin_t = _rope_tables(positions)
    xf = x.reshape(t, d)
    memf = mem.reshape(-1, d)
    row = lambda v: v.reshape(1, -1)

    for l in range(depth):
        lam_init = 0.8 - 0.6 * math.exp(-0.3 * l)
        wi = w_in[l]
        w1 = jnp.concatenate([wi[:, :c0], jnp.zeros((d, LANES - MLA_ROPE_DIM), wi.dtype)], axis=1).astype(BF16)
        w2 = jnp.concatenate([wi[:, c0 + 2 * dqk + dv_w:], wi[:, c0:c0 + 2 * dqk + dv_w]], axis=1).astype(BF16)
        wq = jnp.pad(w_uq[l].reshape(q_rank, MLA_HEADS, MLA_NOPE_DIM + MLA_ROPE_DIM),
                     ((0, 0), (0, 0), (0, MLA_QK_PAD - MLA_NOPE_DIM - MLA_ROPE_DIM)))
        wq = wq.reshape(q_rank, MLA_HEADS * MLA_QK_PAD).astype(BF16)
        wkv = _split_heads_kv(w_ukv[l], MLA_HEADS, MLA_NOPE_DIM).astype(BF16)
        wckv = _split_heads_kv(w_cross_kv[l], XATTN_HEADS, XATTN_HEAD_DIM).astype(BF16)

        q, k, v = _mla_proj(xf, row(g_mix_norm[l]), w1, row(g_q_norm[l]), row(g_kv_norm[l]), wq, wkv,
                            cos_t, sin_t, tm=512)
        gate_blocks = 2 * d // tn
        z = _in_proj(xf, row(g_mix_norm[l]), w2, cos_t, sin_t, tm=1024, tn=tn, gate_blocks=gate_blocks)
        oa = _mla_attn(q, k, v, batch=batch, tq=512)
        col = 2 * d // LANES
        ob = _diff_attn(row(lambda_q1[l]), row(lambda_k1[l]), row(lambda_q2[l]), row(lambda_k2[l]),
                        row(g_diff_sub[l]), z, batch=batch, tq=512,
                        q_col=col, k_col=col + DIFF_HEADS, v_col=col + 2 * DIFF_HEADS, lam_init=lam_init)
        mkv = _mem_kv(memf, row(g_mem_norm[l]), wckv, tm=memf.shape[0] // batch)
        xf = _merge_cross(xf, oa, ob, z, w_o_mla[l].astype(BF16), w_o_diff[l].astype(BF16),
                          w_out[l].astype(BF16), row(g_cross_norm[l]), w_cross_q[l].astype(BF16), mkv,
                          w_cross_o[l].astype(BF16), tm=512, batch=batch)
        xf = _conv_ffn(xf, row(g_ffn_norm[l]), w_up[l].astype(BF16), conv_w[l], row(conv_b[l]),
                       w_down[l].astype(BF16), row(g_final), tm=512, tf=512, batch=batch,
                       final_norm=(l == depth - 1))
    return xf.reshape(batch, seq, d)
```

```python
import functools
import math

import jax
import jax.numpy as jnp
from jax import lax
from jax.experimental import pallas as pl
from jax.experimental.pallas import tpu as pltpu

F32 = jnp.float32
BF16 = jnp.bfloat16

EPS = 1e-6
LOG2_E = math.log2(math.e)
ROPE_THETA = 10000.0
MLA_HEADS = 8
MLA_NOPE_DIM = 128
MLA_ROPE_DIM = 64
MLA_V_DIM = 128
MLA_QK_PAD = 256
DIFF_HEADS = 8
DIFF_HEAD_DIM = 64
DIFF_V_DIM = 2 * DIFF_HEAD_DIM
XATTN_HEADS = 4
XATTN_HEAD_DIM = 128
LANES = 128
BF16_SUBLANES = 16
VMEM_LIMIT_BYTES = 56 * 1024 * 1024

_NT = (((1,), (1,)), ((), ()))


def _params(*semantics):
    return pltpu.CompilerParams(dimension_semantics=semantics, vmem_limit_bytes=VMEM_LIMIT_BYTES)


def _rms(x, g):
    ms = jnp.mean(x * x, axis=-1, keepdims=True)
    return x * lax.rsqrt(ms + EPS) * g


def _sigmoid(x):
    return 1.0 / (1.0 + jnp.exp(-x))


def _rope128(x, cos_t, sin_t):
    lane = lax.broadcasted_iota(jnp.int32, x.shape, 1)
    first_half = (lane & (MLA_ROPE_DIM // 2)) == 0
    partner = jnp.where(first_half, pltpu.roll(x, LANES - 32, 1), pltpu.roll(x, 32, 1))
    return x * cos_t + partner * sin_t


def _softmax_parts(s):
    m = jnp.max(s, axis=-1, keepdims=True)
    e = jnp.exp2(s - m)
    return e, jnp.sum(e, axis=-1, keepdims=True)


def _mla_proj_kernel(x_ref, g_ref, w1_ref, gq_ref, gkv_ref, wq_ref, wkv_ref, cos_ref, sin_ref,
                     q_ref, k_ref, v_ref, *, q_rank, kv_rank, q_scale):
    xn = _rms(x_ref[...], g_ref[...]).astype(BF16)
    z = jnp.dot(xn, w1_ref[...], preferred_element_type=F32)
    cqn = _rms(z[:, :q_rank], gq_ref[...]).astype(BF16)
    ckvn = _rms(z[:, q_rank:q_rank + kv_rank], gkv_ref[...]).astype(BF16)
    cos_t, sin_t = cos_ref[...], sin_ref[...]
    kpe = _rope128(z[:, q_rank + kv_rank:], cos_t, sin_t).astype(BF16)
    q = jnp.dot(cqn, wq_ref[...], preferred_element_type=F32) * q_scale
    kv = jnp.dot(ckvn, wkv_ref[...], preferred_element_type=F32)
    for h in range(MLA_HEADS):
        lo, mid, hi = h * MLA_QK_PAD, h * MLA_QK_PAD + MLA_NOPE_DIM, (h + 1) * MLA_QK_PAD
        q_ref[:, lo:mid] = q[:, lo:mid].astype(BF16)
        q_ref[:, mid:hi] = _rope128(q[:, mid:hi], cos_t, sin_t).astype(BF16)
        k_ref[:, lo:mid] = kv[:, h * MLA_NOPE_DIM:(h + 1) * MLA_NOPE_DIM].astype(BF16)
        k_ref[:, mid:hi] = kpe
    v_ref[...] = kv[:, MLA_HEADS * MLA_NOPE_DIM:].astype(BF16)


def _mla_proj(x, g, w1, gq, gkv, wq, wkv, cos_t, sin_t, *, tm):
    t, d = x.shape
    q_rank, kv_rank = gq.shape[1], gkv.shape[1]
    full = lambda a: pl.BlockSpec(a.shape, lambda i: (0, 0))
    rows = lambda width: pl.BlockSpec((tm, width), lambda i: (i, 0))
    qk_w, v_w = MLA_HEADS * MLA_QK_PAD, MLA_HEADS * MLA_V_DIM
    kern = functools.partial(_mla_proj_kernel, q_rank=q_rank, kv_rank=kv_rank,
                             q_scale=LOG2_E * (MLA_NOPE_DIM + MLA_ROPE_DIM) ** -0.5)
    return pl.pallas_call(
        kern,
        grid=(t // tm,),
        in_specs=[rows(d), full(g), full(w1), full(gq), full(gkv), full(wq), full(wkv),
                  rows(LANES), rows(LANES)],
        out_specs=[rows(qk_w), rows(qk_w), rows(v_w)],
        out_shape=[jax.ShapeDtypeStruct((t, qk_w), BF16), jax.ShapeDtypeStruct((t, qk_w), BF16),
                   jax.ShapeDtypeStruct((t, v_w), BF16)],
        compiler_params=_params("parallel"),
        name="mla_proj",
    )(x, g, w1, gq, gkv, wq, wkv, cos_t, sin_t)


def _in_proj_kernel(x_ref, g_ref, w_ref, cos_ref, sin_ref, o_ref, xn_ref, *, gate_blocks, dq_scale):
    j = pl.program_id(1)

    @pl.when(j == 0)
    def _():
        xn_ref[...] = _rms(x_ref[...], g_ref[...]).astype(BF16)

    z = jnp.dot(xn_ref[...], w_ref[...], preferred_element_type=F32)

    def rope_out(scale):
        cos_t, sin_t = cos_ref[...], sin_ref[...]
        for c in range(z.shape[1] // LANES):
            zc = z[:, c * LANES:(c + 1) * LANES]
            if scale != 1.0:
                zc = zc * scale
            o_ref[:, c * LANES:(c + 1) * LANES] = _rope128(zc, cos_t, sin_t).astype(BF16)

    @pl.when(j < gate_blocks)
    def _():
        o_ref[...] = _sigmoid(z).astype(BF16)

    @pl.when(j == gate_blocks)
    def _():
        rope_out(dq_scale)

    @pl.when(j == gate_blocks + 1)
    def _():
        rope_out(1.0)

    @pl.when(j == gate_blocks + 2)
    def _():
        o_ref[...] = z.astype(BF16)


def _in_proj(x, g, w, cos_t, sin_t, *, tm, tn, gate_blocks):
    t, d = x.shape
    n = w.shape[1]
    kern = functools.partial(_in_proj_kernel, gate_blocks=gate_blocks, dq_scale=LOG2_E * DIFF_HEAD_DIM ** -0.5)
    return pl.pallas_call(
        kern,
        grid=(t // tm, n // tn),
        in_specs=[pl.BlockSpec((tm, d), lambda i, j: (i, 0)),
                  pl.BlockSpec((1, d), lambda i, j: (0, 0)),
                  pl.BlockSpec((d, tn), lambda i, j: (0, j)),
                  pl.BlockSpec((tm, LANES), lambda i, j: (i, 0)),
                  pl.BlockSpec((tm, LANES), lambda i, j: (i, 0))],
        out_specs=pl.BlockSpec((tm, tn), lambda i, j: (i, j)),
        out_shape=jax.ShapeDtypeStruct((t, n), BF16),
        scratch_shapes=[pltpu.VMEM((tm, d), BF16)],
        compiler_params=_params("parallel", "arbitrary"),
        name="in_proj",
    )(x, g, w, cos_t, sin_t)


def _mla_attn_kernel(q_ref, k_ref, v_ref, o_ref, *, tq):
    for r in range(q_ref.shape[0] // tq):
        rows = slice(r * tq, (r + 1) * tq)
        s = lax.dot_general(q_ref[rows, :], k_ref[...], _NT, preferred_element_type=F32)
        e, l = _softmax_parts(s)
        o = jnp.dot(e.astype(BF16), v_ref[...], preferred_element_type=F32) * (1.0 / l)
        o_ref[rows, :] = o.astype(BF16)


def _mla_attn(q, k, v, *, batch, tq):
    t = q.shape[0]
    s = t // batch
    return pl.pallas_call(
        functools.partial(_mla_attn_kernel, tq=tq),
        grid=(batch, MLA_HEADS),
        in_specs=[pl.BlockSpec((s, MLA_QK_PAD), lambda b, h: (b, h)),
                  pl.BlockSpec((s, MLA_QK_PAD), lambda b, h: (b, h)),
                  pl.BlockSpec((s, MLA_V_DIM), lambda b, h: (b, h))],
        out_specs=pl.BlockSpec((s, MLA_V_DIM), lambda b, h: (b, h)),
        out_shape=jax.ShapeDtypeStruct((t, MLA_HEADS * MLA_V_DIM), BF16),
        compiler_params=_params("parallel", "parallel"),
        name="mla_attn",
    )(q, k, v)


def _diff_attn_kernel(lq1_ref, lk1_ref, lq2_ref, lk2_ref, gsub_ref, q_ref, k_ref, v_ref, o_ref,
                      *, tq, lam_init):
    lam = (jnp.exp(jnp.sum(lq1_ref[...] * lk1_ref[...], axis=-1, keepdims=True))
           - jnp.exp(jnp.sum(lq2_ref[...] * lk2_ref[...], axis=-1, keepdims=True)) + lam_init)
    gain = gsub_ref[...] * (1.0 - lam_init)
    lane = lax.broadcasted_iota(jnp.int32, (tq, 2 * DIFF_HEAD_DIM), 1)
    map0 = lane < DIFF_HEAD_DIM

    for r in range(q_ref.shape[0] // tq):
        rows = slice(r * tq, (r + 1) * tq)
        q = q_ref[rows, :]
        zero = jnp.zeros_like(q)
        q01 = jnp.concatenate([jnp.where(map0, q, zero), jnp.where(map0, zero, q)], axis=0)
        s = lax.dot_general(q01, k_ref[...], _NT, preferred_element_type=F32)
        e, l = _softmax_parts(s)
        inv = 1.0 / l
        a = e[:tq] - e[tq:] * (lam * l[:tq] * inv[tq:])
        o = jnp.dot(a.astype(BF16), v_ref[...], preferred_element_type=F32) * inv[:tq]
        o_ref[rows, :] = _rms(o, gain).astype(BF16)


def _diff_attn(lq1, lk1, lq2, lk2, gsub, z, *, batch, tq, q_col, k_col, v_col, lam_init):
    t = z.shape[0]
    s = t // batch
    hw = 2 * DIFF_HEAD_DIM
    small = lambda a: pl.BlockSpec(a.shape, lambda b, h: (0, 0))
    head = lambda col: pl.BlockSpec((s, hw), lambda b, h: (b, col + h))
    return pl.pallas_call(
        functools.partial(_diff_attn_kernel, tq=tq, lam_init=lam_init),
        grid=(batch, DIFF_HEADS),
        in_specs=[small(lq1), small(lk1), small(lq2), small(lk2), small(gsub),
                  head(q_col), head(k_col), head(v_col)],
        out_specs=pl.BlockSpec((s, DIFF_V_DIM), lambda b, h: (b, h)),
        out_shape=jax.ShapeDtypeStruct((t, DIFF_HEADS * DIFF_V_DIM), BF16),
        compiler_params=_params("parallel", "parallel"),
        name="diff_attn",
    )(lq1, lk1, lq2, lk2, gsub, z, z, z)


def _mem_kv_kernel(m_ref, g_ref, w_ref, o_ref):
    mn = _rms(m_ref[...], g_ref[...]).astype(BF16)
    o_ref[...] = jnp.dot(mn, w_ref[...], preferred_element_type=F32).astype(BF16)


def _mem_kv(mem, g, w, *, tm):
    t, d = mem.shape
    n = w.shape[1]
    return pl.pallas_call(
        _mem_kv_kernel,
        grid=(t // tm,),
        in_specs=[pl.BlockSpec((tm, d), lambda i: (i, 0)),
                  pl.BlockSpec((1, d), lambda i: (0, 0)),
                  pl.BlockSpec((d, n), lambda i: (0, 0))],
        out_specs=pl.BlockSpec((tm, n), lambda i: (i, 0)),
        out_shape=jax.ShapeDtypeStruct((t, n), BF16),
        compiler_params=_params("parallel"),
        name="mem_kv",
    )(mem, g, w)


def _merge_cross_kernel(x_ref, oa_ref, ob_ref, gate_ref, woa_ref, wob_ref, wout_ref,
                        gc_ref, wcq_ref, mkv_ref, wco_ref, o_ref, *, q_scale):
    d = x_ref.shape[1]
    ya = jnp.dot(oa_ref[...], woa_ref[...], preferred_element_type=F32)
    yb = jnp.dot(ob_ref[...], wob_ref[...], preferred_element_type=F32)
    merged = gate_ref[:, :d].astype(F32) * ya + gate_ref[:, d:].astype(F32) * yb
    x1 = x_ref[...] + jnp.dot(merged.astype(BF16), wout_ref[...], preferred_element_type=F32)

    hn = _rms(x1, gc_ref[...]).astype(BF16)
    qc = (jnp.dot(hn, wcq_ref[...], preferred_element_type=F32) * q_scale).astype(BF16)
    hd = XATTN_HEAD_DIM
    heads = []
    for h in range(XATTN_HEADS):
        k_h = mkv_ref[:, h * hd:(h + 1) * hd]
        v_h = mkv_ref[:, (XATTN_HEADS + h) * hd:(XATTN_HEADS + h + 1) * hd]
        s = lax.dot_general(qc[:, h * hd:(h + 1) * hd], k_h, _NT, preferred_element_type=F32)
        e, l = _softmax_parts(s)
        o_h = jnp.dot(e.astype(BF16), v_h, preferred_element_type=F32) * (1.0 / l)
        heads.append(o_h.astype(BF16))
    oc = jnp.concatenate(heads, axis=1)
    o_ref[...] = x1 + jnp.dot(oc, wco_ref[...], preferred_element_type=F32)


def _merge_cross(x, oa, ob, z, woa, wob, wout, gc, wcq, mkv, wco, *, tm, batch):
    t, d = x.shape
    blocks_per_seq = t // batch // tm
    m = mkv.shape[0] // batch
    const = lambda a: pl.BlockSpec(a.shape, lambda i: (0, 0), pipeline_mode=pl.Buffered(1))
    rows = lambda width: pl.BlockSpec((tm, width), lambda i: (i, 0))
    return pl.pallas_call(
        functools.partial(_merge_cross_kernel, q_scale=LOG2_E * XATTN_HEAD_DIM ** -0.5),
        grid=(t // tm,),
        in_specs=[rows(d), rows(oa.shape[1]), rows(ob.shape[1]), rows(2 * d),
                  const(woa), const(wob), const(wout), const(gc), const(wcq),
                  pl.BlockSpec((m, mkv.shape[1]), lambda i: (i // blocks_per_seq, 0)),
                  const(wco)],
        out_specs=rows(d),
        out_shape=jax.ShapeDtypeStruct((t, d), F32),
        compiler_params=_params("parallel"),
        name="merge_cross",
    )(x, oa, ob, z, woa, wob, wout, gc, wcq, mkv, wco)


def _conv_ffn_kernel(xp_ref, x_ref, xn_ref, g_ref, wg_ref, wv_ref, cwg_ref, cwv_ref, cbg_ref, cbv_ref,
                     wd_ref, gfin_ref, o_ref, h_ref, ug_ref, uv_ref, acc_ref,
                     *, tm, halo, blocks_per_seq, final_norm):
    i, f = pl.program_id(0), pl.program_id(1)

    @pl.when(f == 0)
    def _():
        g = g_ref[...]
        pos = i % blocks_per_seq
        keep_prev = jnp.where(pos == 0, 0.0, 1.0)
        keep_next = jnp.where(pos == blocks_per_seq - 1, 0.0, 1.0)
        h_ref[0:halo, :] = (_rms(xp_ref[...], g) * keep_prev).astype(BF16)
        h_ref[halo:halo + tm, :] = _rms(x_ref[...], g).astype(BF16)
        h_ref[halo + tm:, :] = (_rms(xn_ref[...], g) * keep_next).astype(BF16)
        acc_ref[...] = jnp.zeros_like(acc_ref)

    h = h_ref[...]
    ug_ref[...] = jnp.dot(h, wg_ref[...], preferred_element_type=F32)
    uv_ref[...] = jnp.dot(h, wv_ref[...], preferred_element_type=F32)

    def conv(u_ref, cw_ref, cb_ref):
        cw = cw_ref[...]
        return (u_ref[halo - 1:halo - 1 + tm, :] * cw[0:1, :] + u_ref[halo:halo + tm, :] * cw[1:2, :]
                + u_ref[halo + 1:halo + 1 + tm, :] * cw[2:3, :] + cb_ref[...])

    yg = conv(ug_ref, cwg_ref, cbg_ref)
    yv = conv(uv_ref, cwv_ref, cbv_ref)
    act = (yg * _sigmoid(yg) * yv).astype(BF16)
    acc_ref[...] += jnp.dot(act, wd_ref[...], preferred_element_type=F32)

    @pl.when(f == pl.num_programs(1) - 1)
    def _():
        y = x_ref[...] + acc_ref[...]
        o_ref[...] = _rms(y, gfin_ref[...]) if final_norm else y


def _conv_ffn(x, g, w_up, conv_w, conv_b, w_down, g_final, *, tm, tf, batch, final_norm):
    t, d = x.shape
    ffn = w_down.shape[0]
    nf = ffn // tf
    halo = BF16_SUBLANES
    blocks_per_seq = t // batch // tm
    hb = tm // halo
    last_halo_block = t // halo - 1
    kern = functools.partial(_conv_ffn_kernel, tm=tm, halo=halo, blocks_per_seq=blocks_per_seq,
                             final_norm=final_norm)
    return pl.pallas_call(
        kern,
        grid=(t // tm, nf),
        in_specs=[pl.BlockSpec((halo, d), lambda i, f: (jnp.maximum(i * hb - 1, 0), 0)),
                  pl.BlockSpec((tm, d), lambda i, f: (i, 0)),
                  pl.BlockSpec((halo, d), lambda i, f: (jnp.minimum((i + 1) * hb, last_halo_block), 0)),
                  pl.BlockSpec((1, d), lambda i, f: (0, 0)),
                  pl.BlockSpec((d, tf), lambda i, f: (0, f)),
                  pl.BlockSpec((d, tf), lambda i, f: (0, nf + f)),
                  pl.BlockSpec((conv_w.shape[0], tf), lambda i, f: (0, f)),
                  pl.BlockSpec((conv_w.shape[0], tf), lambda i, f: (0, nf + f)),
                  pl.BlockSpec((1, tf), lambda i, f: (0, f)),
                  pl.BlockSpec((1, tf), lambda i, f: (0, nf + f)),
                  pl.BlockSpec((tf, d), lambda i, f: (f, 0)),
                  pl.BlockSpec((1, d), lambda i, f: (0, 0))],
        out_specs=pl.BlockSpec((tm, d), lambda i, f: (i, 0)),
        out_shape=jax.ShapeDtypeStruct((t, d), F32),
        scratch_shapes=[pltpu.VMEM((tm + 2 * halo, d), BF16),
                        pltpu.VMEM((tm + 2 * halo, tf), F32),
                        pltpu.VMEM((tm + 2 * halo, tf), F32),
                        pltpu.VMEM((tm, d), F32)],
        compiler_params=_params("parallel", "arbitrary"),
        name="conv_ffn",
    )(x, x, x, g, w_up, w_up, conv_w, conv_w, conv_b, conv_b, w_down, g_final)


def _rope_tables(positions):
    dim = MLA_ROPE_DIM
    inv = ROPE_THETA ** (-jnp.arange(0, dim, 2, dtype=F32) / dim)
    ang = positions.astype(F32)[..., None] * inv
    cos, sin = jnp.cos(ang), jnp.sin(ang)
    reps = LANES // dim
    cos_t = jnp.concatenate([cos, cos] * reps, axis=-1)
    sin_t = jnp.concatenate([-sin, sin] * reps, axis=-1)
    return cos_t.reshape(-1, LANES), sin_t.reshape(-1, LANES)


def _split_heads_kv(w, heads, dim):
    r = w.shape[0]
    return w.reshape(r, heads, 2, dim).transpose(0, 2, 1, 3).reshape(r, 2 * heads * dim)


def kernel(x, mem, positions, g_mix_norm, w_in, g_q_norm, w_uq, g_kv_norm, w_ukv, w_o_mla, lambda_q1, lambda_k1, lambda_q2, lambda_k2, g_diff_sub, w_o_diff, w_out, g_cross_norm, g_mem_norm, w_cross_q, w_cross_kv, w_cross_o, g_ffn_norm, w_up, conv_w, conv_b, w_down, g_final):
    batch, seq, d = x.shape
    depth = w_in.shape[0]
    t = batch * seq
    assert depth >= 1 and MLA_ROPE_DIM == DIFF_HEAD_DIM and MLA_NOPE_DIM == MLA_V_DIM
    q_rank, kv_rank = g_q_norm.shape[1], g_kv_norm.shape[1]
    dqk = DIFF_HEADS * 2 * DIFF_HEAD_DIM
    dv_w = DIFF_HEADS * DIFF_V_DIM
    c0 = q_rank + kv_rank + MLA_ROPE_DIM
    tn = 1024
    assert dqk == tn and dv_w == tn and (2 * d) % tn == 0

    cos_t, sin_t = _rope_tables(positions)
    xf = x.reshape(t, d)
    memf = mem.reshape(-1, d)
    row = lambda v: v.reshape(1, -1)

    for l in range(depth):
        lam_init = 0.8 - 0.6 * math.exp(-0.3 * l)
        wi = w_in[l]
        w1 = jnp.concatenate([wi[:, :c0], jnp.zeros((d, LANES - MLA_ROPE_DIM), wi.dtype)], axis=1).astype(BF16)
        w2 = jnp.concatenate([wi[:, c0 + 2 * dqk + dv_w:], wi[:, c0:c0 + 2 * dqk + dv_w]], axis=1).astype(BF16)
        wq = jnp.pad(w_uq[l].reshape(q_rank, MLA_HEADS, MLA_NOPE_DIM + MLA_ROPE_DIM),
                     ((0, 0), (0, 0), (0, MLA_QK_PAD - MLA_NOPE_DIM - MLA_ROPE_DIM)))
        wq = wq.reshape(q_rank, MLA_HEADS * MLA_QK_PAD).astype(BF16)
        wkv = _split_heads_kv(w_ukv[l], MLA_HEADS, MLA_NOPE_DIM).astype(BF16)
        wckv = _split_heads_kv(w_cross_kv[l], XATTN_HEADS, XATTN_HEAD_DIM).astype(BF16)

        q, k, v = _mla_proj(xf, row(g_mix_norm[l]), w1, row(g_q_norm[l]), row(g_kv_norm[l]), wq, wkv,
                            cos_t, sin_t, tm=512)
        gate_blocks = 2 * d // tn
        z = _in_proj(xf, row(g_mix_norm[l]), w2, cos_t, sin_t, tm=1024, tn=tn, gate_blocks=gate_blocks)
        oa = _mla_attn(q, k, v, batch=batch, tq=512)
        col = 2 * d // LANES
        ob = _diff_attn(row(lambda_q1[l]), row(lambda_k1[l]), row(lambda_q2[l]), row(lambda_k2[l]),
                        row(g_diff_sub[l]), z, batch=batch, tq=512,
                        q_col=col, k_col=col + DIFF_HEADS, v_col=col + 2 * DIFF_HEADS, lam_init=lam_init)
        mkv = _mem_kv(memf, row(g_mem_norm[l]), wckv, tm=memf.shape[0] // batch)
        xf = _merge_cross(xf, oa, ob, z, w_o_mla[l].astype(BF16), w_o_diff[l].astype(BF16),
                          w_out[l].astype(BF16), row(g_cross_norm[l]), w_cross_q[l].astype(BF16), mkv,
                          w_cross_o[l].astype(BF16), tm=512, batch=batch)
        xf = _conv_ffn(xf, row(g_ffn_norm[l]), w_up[l].astype(BF16), conv_w[l], row(conv_b[l]),
                       w_down[l].astype(BF16), row(g_final), tm=512, tf=512, batch=batch,
                       final_norm=(l == depth - 1))
    return xf.reshape(batch, seq, d)
```

```python
import functools
import math

import jax
import jax.numpy as jnp
from jax import lax
from jax.experimental import pallas as pl
from jax.experimental.pallas import tpu as pltpu

F32 = jnp.float32
BF16 = jnp.bfloat16

EPS = 1e-6
LOG2_E = math.log2(math.e)
ROPE_THETA = 10000.0
MLA_HEADS = 8
MLA_NOPE_DIM = 128
MLA_ROPE_DIM = 64
MLA_V_DIM = 128
MLA_QK_PAD = 256
DIFF_HEADS = 8
DIFF_HEAD_DIM = 64
DIFF_V_DIM = 2 * DIFF_HEAD_DIM
XATTN_HEADS = 4
XATTN_HEAD_DIM = 128
LANES = 128
BF16_SUBLANES = 16
MXU_COLS = 256
VMEM_LIMIT_BYTES = 56 * 1024 * 1024

_NT = (((1,), (1,)), ((), ()))


def _params(*semantics):
    return pltpu.CompilerParams(dimension_semantics=semantics, vmem_limit_bytes=VMEM_LIMIT_BYTES)


def _rms(x, g):
    ms = jnp.mean(x * x, axis=-1, keepdims=True)
    return x * lax.rsqrt(ms + EPS) * g


def _sigmoid(x):
    return 1.0 / (1.0 + jnp.exp(-x))


def _rope128(x, cos_t, sin_t):
    lane = lax.broadcasted_iota(jnp.int32, x.shape, 1)
    first_half = (lane & (MLA_ROPE_DIM // 2)) == 0
    partner = jnp.where(first_half, pltpu.roll(x, LANES - 32, 1), pltpu.roll(x, 32, 1))
    return x * cos_t + partner * sin_t


def _softmax_parts(s):
    m = jnp.max(s, axis=-1, keepdims=True)
    e = jnp.exp2(s - m)
    return e, jnp.sum(e, axis=-1, keepdims=True)


def _mla_proj_kernel(x_ref, g_ref, w1_ref, gq_ref, gkv_ref, wq_ref, wkv_ref, cos_ref, sin_ref,
                     q_ref, k_ref, v_ref, xn_ref, *, q_rank, kv_rank, q_scale):
    xn = _rms(x_ref[...], g_ref[...]).astype(BF16)
    xn_ref[...] = xn
    z = jnp.dot(xn, w1_ref[...], preferred_element_type=F32)
    cqn = _rms(z[:, :q_rank], gq_ref[...]).astype(BF16)
    ckvn = _rms(z[:, q_rank:q_rank + kv_rank], gkv_ref[...]).astype(BF16)
    cos_t, sin_t = cos_ref[...], sin_ref[...]
    kpe = _rope128(z[:, q_rank + kv_rank:], cos_t, sin_t).astype(BF16)
    q = jnp.dot(cqn, wq_ref[...], preferred_element_type=F32) * q_scale
    kv = jnp.dot(ckvn, wkv_ref[...], preferred_element_type=F32)
    for h in range(MLA_HEADS):
        lo, mid, hi = h * MLA_QK_PAD, h * MLA_QK_PAD + MLA_NOPE_DIM, (h + 1) * MLA_QK_PAD
        q_ref[:, lo:mid] = q[:, lo:mid].astype(BF16)
        q_ref[:, mid:hi] = _rope128(q[:, mid:hi], cos_t, sin_t).astype(BF16)
        k_ref[:, lo:mid] = kv[:, h * MLA_NOPE_DIM:(h + 1) * MLA_NOPE_DIM].astype(BF16)
        k_ref[:, mid:hi] = kpe
    v_ref[...] = kv[:, MLA_HEADS * MLA_NOPE_DIM:].astype(BF16)


def _mla_proj(x, g, w1, gq, gkv, wq, wkv, cos_t, sin_t, *, tm):
    t, d = x.shape
    q_rank, kv_rank = gq.shape[1], gkv.shape[1]
    full = lambda a: pl.BlockSpec(a.shape, lambda i: (0, 0))
    rows = lambda width: pl.BlockSpec((tm, width), lambda i: (i, 0))
    qk_w, v_w = MLA_HEADS * MLA_QK_PAD, MLA_HEADS * MLA_V_DIM
    kern = functools.partial(_mla_proj_kernel, q_rank=q_rank, kv_rank=kv_rank,
                             q_scale=LOG2_E * (MLA_NOPE_DIM + MLA_ROPE_DIM) ** -0.5)
    return pl.pallas_call(
        kern,
        grid=(t // tm,),
        in_specs=[rows(d), full(g), full(w1), full(gq), full(gkv), full(wq), full(wkv),
                  rows(LANES), rows(LANES)],
        out_specs=[rows(qk_w), rows(qk_w), rows(v_w), rows(d)],
        out_shape=[jax.ShapeDtypeStruct((t, qk_w), BF16), jax.ShapeDtypeStruct((t, qk_w), BF16),
                   jax.ShapeDtypeStruct((t, v_w), BF16), jax.ShapeDtypeStruct((t, d), BF16)],
        compiler_params=_params("parallel"),
        name="mla_proj",
    )(x, g, w1, gq, gkv, wq, wkv, cos_t, sin_t)


def _qkv_proj_kernel(xn_ref, w_ref, cos_ref, sin_ref, o_ref, *, dq_scale):
    j = pl.program_id(1)
    sub = range(0, o_ref.shape[1], MXU_COLS)

    def z_cols(c):
        return jnp.dot(xn_ref[...], w_ref[:, c:c + MXU_COLS], preferred_element_type=F32)

    def rope_out(scale):
        cos_t, sin_t = cos_ref[...], sin_ref[...]
        for c in sub:
            z = z_cols(c)
            for cc in range(0, MXU_COLS, LANES):
                zc = z[:, cc:cc + LANES]
                if scale != 1.0:
                    zc = zc * scale
                o_ref[:, c + cc:c + cc + LANES] = _rope128(zc, cos_t, sin_t).astype(BF16)

    @pl.when(j == 0)
    def _():
        rope_out(dq_scale)

    @pl.when(j == 1)
    def _():
        rope_out(1.0)

    @pl.when(j == 2)
    def _():
        for c in sub:
            o_ref[:, c:c + MXU_COLS] = z_cols(c).astype(BF16)


def _gate_proj_kernel(xn_ref, w_ref, o_ref):
    for c in range(0, o_ref.shape[1], MXU_COLS):
        z = jnp.dot(xn_ref[...], w_ref[:, c:c + MXU_COLS], preferred_element_type=F32)
        o_ref[:, c:c + MXU_COLS] = _sigmoid(z).astype(BF16)


def _qkv_proj(xn, w, cos_t, sin_t, *, tm, tn):
    t, d = xn.shape
    kern = functools.partial(_qkv_proj_kernel, dq_scale=LOG2_E * DIFF_HEAD_DIM ** -0.5)
    return pl.pallas_call(
        kern,
        grid=(t // tm, 3),
        in_specs=[pl.BlockSpec((tm, d), lambda i, j: (i, 0)),
                  pl.BlockSpec((d, tn), lambda i, j: (0, j)),
                  pl.BlockSpec((tm, LANES), lambda i, j: (i, 0)),
                  pl.BlockSpec((tm, LANES), lambda i, j: (i, 0))],
        out_specs=pl.BlockSpec((tm, tn), lambda i, j: (i, j)),
        out_shape=jax.ShapeDtypeStruct((t, 3 * tn), BF16),
        compiler_params=_params("parallel", "arbitrary"),
        name="qkv_proj",
    )(xn, w, cos_t, sin_t)


def _gate_proj(xn, w, *, tm, tn, first_block):
    t, d = xn.shape
    nb = w.shape[1] // tn - first_block
    return pl.pallas_call(
        _gate_proj_kernel,
        grid=(t // tm, nb),
        in_specs=[pl.BlockSpec((tm, d), lambda i, j: (i, 0)),
                  pl.BlockSpec((d, tn), lambda i, j: (0, first_block + j))],
        out_specs=pl.BlockSpec((tm, tn), lambda i, j: (i, j)),
        out_shape=jax.ShapeDtypeStruct((t, nb * tn), BF16),
        compiler_params=_params("parallel", "arbitrary"),
        name="gate_proj",
    )(xn, w)


def _rows(c, tq):
    if isinstance(c, int):
        return slice(c * tq, (c + 1) * tq)
    return pl.ds(pl.multiple_of(c * tq, tq), tq)


def _pipelined_chunks(n, qk, sm, pv):
    def step(c, parity):
        qk(c, parity)
        sm(1 - parity)
        pv(c - 2, parity)

    qk(0, 0)
    qk(1, 1)
    sm(0)

    def body(i, carry):
        step(2 * i, 0)
        step(2 * i + 1, 1)
        return carry

    lax.fori_loop(1, n // 2, body, 0)
    sm(1)
    pv(n - 2, 0)
    pv(n - 1, 1)


def _mla_attn_kernel(q_ref, k_ref, v_ref, o_ref, s0, s1, p0, p1, l0, l1, *, tq):
    sb, pb, lb = (s0, s1), (p0, p1), (l0, l1)

    def qk(c, slot):
        sb[slot][...] = lax.dot_general(q_ref[_rows(c, tq), :], k_ref[...], _NT, preferred_element_type=F32)

    def sm(slot):
        e, l = _softmax_parts(sb[slot][...])
        lb[slot][...] = 1.0 / l
        pb[slot][...] = e.astype(BF16)

    def pv(c, slot):
        o = jnp.dot(pb[slot][...], v_ref[...], preferred_element_type=F32) * lb[slot][...]
        o_ref[_rows(c, tq), :] = o.astype(BF16)

    _pipelined_chunks(q_ref.shape[0] // tq, qk, sm, pv)


def _mla_attn(q, k, v, *, batch, tq):
    t = q.shape[0]
    s = t // batch
    return pl.pallas_call(
        functools.partial(_mla_attn_kernel, tq=tq),
        grid=(batch, MLA_HEADS),
        in_specs=[pl.BlockSpec((s, MLA_QK_PAD), lambda b, h: (b, h)),
                  pl.BlockSpec((s, MLA_QK_PAD), lambda b, h: (b, h)),
                  pl.BlockSpec((s, MLA_V_DIM), lambda b, h: (b, h))],
        out_specs=pl.BlockSpec((s, MLA_V_DIM), lambda b, h: (b, h)),
        out_shape=jax.ShapeDtypeStruct((t, MLA_HEADS * MLA_V_DIM), BF16),
        scratch_shapes=[pltpu.VMEM((tq, s), F32), pltpu.VMEM((tq, s), F32),
                        pltpu.VMEM((tq, s), BF16), pltpu.VMEM((tq, s), BF16),
                        pltpu.VMEM((tq, 1), F32), pltpu.VMEM((tq, 1), F32)],
        compiler_params=_params("parallel", "parallel"),
        name="mla_attn",
    )(q, k, v)


def _diff_attn_kernel(lq1_ref, lk1_ref, lq2_ref, lk2_ref, gsub_ref, q_ref, k_ref, v_ref, o_ref,
                      s0, s1, p0, p1, l0, l1, *, tq, lam_init):
    lam = (jnp.exp(jnp.sum(lq1_ref[...] * lk1_ref[...], axis=-1, keepdims=True))
           - jnp.exp(jnp.sum(lq2_ref[...] * lk2_ref[...], axis=-1, keepdims=True)) + lam_init)
    gain = gsub_ref[...] * (1.0 - lam_init)
    lane = lax.broadcasted_iota(jnp.int32, (tq, 2 * DIFF_HEAD_DIM), 1)
    map0 = lane < DIFF_HEAD_DIM
    sb, pb, lb = (s0, s1), (p0, p1), (l0, l1)

    def qk(c, slot):
        q = q_ref[_rows(c, tq), :]
        zero = jnp.zeros_like(q)
        q01 = jnp.concatenate([jnp.where(map0, q, zero), jnp.where(map0, zero, q)], axis=0)
        sb[slot][...] = lax.dot_general(q01, k_ref[...], _NT, preferred_element_type=F32)

    def sm(slot):
        e, l = _softmax_parts(sb[slot][...])
        inv = 1.0 / l
        lb[slot][...] = inv[:tq]
        pb[slot][...] = (e[:tq] - e[tq:] * (lam * l[:tq] * inv[tq:])).astype(BF16)

    def pv(c, slot):
        o = jnp.dot(pb[slot][...], v_ref[...], preferred_element_type=F32) * lb[slot][...]
        o_ref[_rows(c, tq), :] = _rms(o, gain).astype(BF16)

    _pipelined_chunks(q_ref.shape[0] // tq, qk, sm, pv)


def _diff_attn(lq1, lk1, lq2, lk2, gsub, z, *, batch, tq, q_col, k_col, v_col, lam_init):
    t = z.shape[0]
    s = t // batch
    hw = 2 * DIFF_HEAD_DIM
    small = lambda a: pl.BlockSpec(a.shape, lambda b, h: (0, 0))
    head = lambda col: pl.BlockSpec((s, hw), lambda b, h: (b, col + h))
    return pl.pallas_call(
        functools.partial(_diff_attn_kernel, tq=tq, lam_init=lam_init),
        grid=(batch, DIFF_HEADS),
        in_specs=[small(lq1), small(lk1), small(lq2), small(lk2), small(gsub),
                  head(q_col), head(k_col), head(v_col)],
        out_specs=pl.BlockSpec((s, DIFF_V_DIM), lambda b, h: (b, h)),
        out_shape=jax.ShapeDtypeStruct((t, DIFF_HEADS * DIFF_V_DIM), BF16),
        scratch_shapes=[pltpu.VMEM((2 * tq, s), F32), pltpu.VMEM((2 * tq, s), F32),
                        pltpu.VMEM((tq, s), BF16), pltpu.VMEM((tq, s), BF16),
                        pltpu.VMEM((tq, 1), F32), pltpu.VMEM((tq, 1), F32)],
        compiler_params=_params("parallel", "parallel"),
        name="diff_attn",
    )(lq1, lk1, lq2, lk2, gsub, z, z, z)


def _mem_kv_kernel(m_ref, g_ref, w_ref, o_ref):
    mn = _rms(m_ref[...], g_ref[...]).astype(BF16)
    o_ref[...] = jnp.dot(mn, w_ref[...], preferred_element_type=F32).astype(BF16)


def _mem_kv(mem, g, w, *, tm):
    t, d = mem.shape
    n = w.shape[1]
    return pl.pallas_call(
        _mem_kv_kernel,
        grid=(t // tm,),
        in_specs=[pl.BlockSpec((tm, d), lambda i: (i, 0)),
                  pl.BlockSpec((1, d), lambda i: (0, 0)),
                  pl.BlockSpec((d, n), lambda i: (0, 0))],
        out_specs=pl.BlockSpec((tm, n), lambda i: (i, 0)),
        out_shape=jax.ShapeDtypeStruct((t, n), BF16),
        compiler_params=_params("parallel"),
        name="mem_kv",
    )(mem, g, w)


def _merge_cross_kernel(x_ref, oa_ref, ob_ref, gate_ref, woa_ref, wob_ref, wout_ref,
                        gc_ref, wcq_ref, mkv_ref, wco_ref, o_ref, *, q_scale):
    d = x_ref.shape[1]
    ya = jnp.dot(oa_ref[...], woa_ref[...], preferred_element_type=F32)
    yb = jnp.dot(ob_ref[...], wob_ref[...], preferred_element_type=F32)
    merged = gate_ref[:, :d].astype(F32) * ya + gate_ref[:, d:].astype(F32) * yb
    x1 = x_ref[...] + jnp.dot(merged.astype(BF16), wout_ref[...], preferred_element_type=F32)

    hn = _rms(x1, gc_ref[...]).astype(BF16)
    qc = (jnp.dot(hn, wcq_ref[...], preferred_element_type=F32) * q_scale).astype(BF16)
    hd = XATTN_HEAD_DIM
    heads = []
    for h in range(XATTN_HEADS):
        k_h = mkv_ref[:, h * hd:(h + 1) * hd]
        v_h = mkv_ref[:, (XATTN_HEADS + h) * hd:(XATTN_HEADS + h + 1) * hd]
        s = lax.dot_general(qc[:, h * hd:(h + 1) * hd], k_h, _NT, preferred_element_type=F32)
        e, l = _softmax_parts(s)
        o_h = jnp.dot(e.astype(BF16), v_h, preferred_element_type=F32) * (1.0 / l)
        heads.append(o_h.astype(BF16))
    oc = jnp.concatenate(heads, axis=1)
    o_ref[...] = x1 + jnp.dot(oc, wco_ref[...], preferred_element_type=F32)


def _merge_cross(x, oa, ob, gates, woa, wob, wout, gc, wcq, mkv, wco, *, tm, batch):
    t, d = x.shape
    blocks_per_seq = t // batch // tm
    m = mkv.shape[0] // batch
    const = lambda a: pl.BlockSpec(a.shape, lambda i: (0, 0), pipeline_mode=pl.Buffered(1))
    rows = lambda width: pl.BlockSpec((tm, width), lambda i: (i, 0))
    return pl.pallas_call(
        functools.partial(_merge_cross_kernel, q_scale=LOG2_E * XATTN_HEAD_DIM ** -0.5),
        grid=(t // tm,),
        in_specs=[rows(d), rows(oa.shape[1]), rows(ob.shape[1]), rows(2 * d),
                  const(woa), const(wob), const(wout), const(gc), const(wcq),
                  pl.BlockSpec((m, mkv.shape[1]), lambda i: (i // blocks_per_seq, 0)),
                  const(wco)],
        out_specs=rows(d),
        out_shape=jax.ShapeDtypeStruct((t, d), F32),
        compiler_params=_params("parallel"),
        name="merge_cross",
    )(x, oa, ob, gates, woa, wob, wout, gc, wcq, mkv, wco)


def _conv_ffn_kernel(xp_ref, x_ref, xn_ref, g_ref, wg_ref, wv_ref, cwg_ref, cwv_ref, cbg_ref, cbv_ref,
                     wd_ref, gfin_ref, o_ref, h_ref, ug_ref, uv_ref, acc_ref,
                     *, tm, halo, blocks_per_seq, final_norm):
    i, f = pl.program_id(0), pl.program_id(1)

    @pl.when(f == 0)
    def _():
        g = g_ref[...]
        pos = i % blocks_per_seq
        keep_prev = jnp.where(pos == 0, 0.0, 1.0)
        keep_next = jnp.where(pos == blocks_per_seq - 1, 0.0, 1.0)
        h_ref[0:halo, :] = (_rms(xp_ref[...], g) * keep_prev).astype(BF16)
        h_ref[halo:halo + tm, :] = _rms(x_ref[...], g).astype(BF16)
        h_ref[halo + tm:, :] = (_rms(xn_ref[...], g) * keep_next).astype(BF16)
        acc_ref[...] = jnp.zeros_like(acc_ref)

    h = h_ref[...]
    ug_ref[...] = jnp.dot(h, wg_ref[...], preferred_element_type=F32)
    uv_ref[...] = jnp.dot(h, wv_ref[...], preferred_element_type=F32)

    def conv(u_ref, cw_ref, cb_ref):
        cw = cw_ref[...]
        return (u_ref[halo - 1:halo - 1 + tm, :] * cw[0:1, :] + u_ref[halo:halo + tm, :] * cw[1:2, :]
                + u_ref[halo + 1:halo + 1 + tm, :] * cw[2:3, :] + cb_ref[...])

    yg = conv(ug_ref, cwg_ref, cbg_ref)
    yv = conv(uv_ref, cwv_ref, cbv_ref)
    act = (yg * _sigmoid(yg) * yv).astype(BF16)
    acc_ref[...] += jnp.dot(act, wd_ref[...], preferred_element_type=F32)

    @pl.when(f == pl.num_programs(1) - 1)
    def _():
        y = x_ref[...] + acc_ref[...]
        o_ref[...] = _rms(y, gfin_ref[...]) if final_norm else y


def _conv_ffn(x, g, w_up, conv_w, conv_b, w_down, g_final, *, tm, tf, batch, final_norm):
    t, d = x.shape
    ffn = w_down.shape[0]
    nf = ffn // tf
    halo = BF16_SUBLANES
    blocks_per_seq = t // batch // tm
    hb = tm // halo
    last_halo_block = t // halo - 1
    kern = functools.partial(_conv_ffn_kernel, tm=tm, halo=halo, blocks_per_seq=blocks_per_seq,
                             final_norm=final_norm)
    return pl.pallas_call(
        kern,
        grid=(t // tm, nf),
        in_specs=[pl.BlockSpec((halo, d), lambda i, f: (jnp.maximum(i * hb - 1, 0), 0)),
                  pl.BlockSpec((tm, d), lambda i, f: (i, 0)),
                  pl.BlockSpec((halo, d), lambda i, f: (jnp.minimum((i + 1) * hb, last_halo_block), 0)),
                  pl.BlockSpec((1, d), lambda i, f: (0, 0)),
                  pl.BlockSpec((d, tf), lambda i, f: (0, f)),
                  pl.BlockSpec((d, tf), lambda i, f: (0, nf + f)),
                  pl.BlockSpec((conv_w.shape[0], tf), lambda i, f: (0, f)),
                  pl.BlockSpec((conv_w.shape[0], tf), lambda i, f: (0, nf + f)),
                  pl.BlockSpec((1, tf), lambda i, f: (0, f)),
                  pl.BlockSpec((1, tf), lambda i, f: (0, nf + f)),
                  pl.BlockSpec((tf, d), lambda i, f: (f, 0)),
                  pl.BlockSpec((1, d), lambda i, f: (0, 0))],
        out_specs=pl.BlockSpec((tm, d), lambda i, f: (i, 0)),
        out_shape=jax.ShapeDtypeStruct((t, d), F32),
        scratch_shapes=[pltpu.VMEM((tm + 2 * halo, d), BF16),
                        pltpu.VMEM((tm + 2 * halo, tf), F32),
                        pltpu.VMEM((tm + 2 * halo, tf), F32),
                        pltpu.VMEM((tm, d), F32)],
        compiler_params=_params("parallel", "arbitrary"),
        name="conv_ffn",
    )(x, x, x, g, w_up, w_up, conv_w, conv_w, conv_b, conv_b, w_down, g_final)


def _rope_tables(positions):
    dim = MLA_ROPE_DIM
    inv = ROPE_THETA ** (-jnp.arange(0, dim, 2, dtype=F32) / dim)
    ang = positions.astype(F32)[..., None] * inv
    cos, sin = jnp.cos(ang), jnp.sin(ang)
    reps = LANES // dim
    cos_t = jnp.concatenate([cos, cos] * reps, axis=-1)
    sin_t = jnp.concatenate([-sin, sin] * reps, axis=-1)
    return cos_t.reshape(-1, LANES), sin_t.reshape(-1, LANES)


def _split_heads_kv(w, heads, dim):
    r = w.shape[0]
    return w.reshape(r, heads, 2, dim).transpose(0, 2, 1, 3).reshape(r, 2 * heads * dim)


def kernel(x, mem, positions, g_mix_norm, w_in, g_q_norm, w_uq, g_kv_norm, w_ukv, w_o_mla, lambda_q1, lambda_k1, lambda_q2, lambda_k2, g_diff_sub, w_o_diff, w_out, g_cross_norm, g_mem_norm, w_cross_q, w_cross_kv, w_cross_o, g_ffn_norm, w_up, conv_w, conv_b, w_down, g_final):
    batch, seq, d = x.shape
    depth = w_in.shape[0]
    t = batch * seq
    assert depth >= 1 and MLA_ROPE_DIM == DIFF_HEAD_DIM and MLA_NOPE_DIM == MLA_V_DIM
    q_rank, kv_rank = g_q_norm.shape[1], g_kv_norm.shape[1]
    dqk = DIFF_HEADS * 2 * DIFF_HEAD_DIM
    dv_w = DIFF_HEADS * DIFF_V_DIM
    c0 = q_rank + kv_rank + MLA_ROPE_DIM
    tn = 1024
    assert dqk == tn and dv_w == tn and (2 * d) % tn == 0

    cos_t, sin_t = _rope_tables(positions)
    xf = x.reshape(t, d)
    memf = mem.reshape(-1, d)
    row = lambda v: v.reshape(1, -1)

    for l in range(depth):
        lam_init = 0.8 - 0.6 * math.exp(-0.3 * l)
        wi = w_in[l]
        w1 = wi[:, :c0 + LANES - MLA_ROPE_DIM].astype(BF16)
        w2 = wi[:, c0:].astype(BF16)
        wq = jnp.pad(w_uq[l].reshape(q_rank, MLA_HEADS, MLA_NOPE_DIM + MLA_ROPE_DIM),
                     ((0, 0), (0, 0), (0, MLA_QK_PAD - MLA_NOPE_DIM - MLA_ROPE_DIM)))
        wq = wq.reshape(q_rank, MLA_HEADS * MLA_QK_PAD).astype(BF16)
        wkv = _split_heads_kv(w_ukv[l], MLA_HEADS, MLA_NOPE_DIM).astype(BF16)
        wckv = _split_heads_kv(w_cross_kv[l], XATTN_HEADS, XATTN_HEAD_DIM).astype(BF16)

        q, k, v, xn = _mla_proj(xf, row(g_mix_norm[l]), w1, row(g_q_norm[l]), row(g_kv_norm[l]), wq, wkv,
                                cos_t, sin_t, tm=512)
        z = _qkv_proj(xn, w2, cos_t, sin_t, tm=1024, tn=tn)
        gates = _gate_proj(xn, w2, tm=1024, tn=tn, first_block=3)
        oa = _mla_attn(q, k, v, batch=batch, tq=512)
        ob = _diff_attn(row(lambda_q1[l]), row(lambda_k1[l]), row(lambda_q2[l]), row(lambda_k2[l]),
                        row(g_diff_sub[l]), z, batch=batch, tq=512,
                        q_col=0, k_col=DIFF_HEADS, v_col=2 * DIFF_HEADS, lam_init=lam_init)
        mkv = _mem_kv(memf, row(g_mem_norm[l]), wckv, tm=memf.shape[0] // batch)
        xf = _merge_cross(xf, oa, ob, gates, w_o_mla[l].astype(BF16), w_o_diff[l].astype(BF16),
                          w_out[l].astype(BF16), row(g_cross_norm[l]), w_cross_q[l].astype(BF16), mkv,
                          w_cross_o[l].astype(BF16), tm=512, batch=batch)
        xf = _conv_ffn(xf, row(g_ffn_norm[l]), w_up[l].astype(BF16), conv_w[l], row(conv_b[l]),
                       w_down[l].astype(BF16), row(g_final), tm=512, tf=512, batch=batch,
                       final_norm=(l == depth - 1))
    return xf.reshape(batch, seq, d)
```

```python
import functools
import math

import jax
import jax.numpy as jnp
from jax import lax
from jax.experimental import pallas as pl
from jax.experimental.pallas import tpu as pltpu

F32 = jnp.float32
BF16 = jnp.bfloat16

EPS = 1e-6
LOG2_E = math.log2(math.e)
ROPE_THETA = 10000.0
MLA_HEADS = 8
MLA_NOPE_DIM = 128
MLA_ROPE_DIM = 64
MLA_V_DIM = 128
MLA_QK_PAD = 256
DIFF_HEADS = 8
DIFF_HEAD_DIM = 64
DIFF_V_DIM = 2 * DIFF_HEAD_DIM
XATTN_HEADS = 4
XATTN_HEAD_DIM = 128
LANES = 128
BF16_SUBLANES = 16
MXU_COLS = 256
VMEM_LIMIT_BYTES = 56 * 1024 * 1024

_NT = (((1,), (1,)), ((), ()))


def _params(*semantics):
    return pltpu.CompilerParams(dimension_semantics=semantics, vmem_limit_bytes=VMEM_LIMIT_BYTES)


def _rms(x, g):
    ms = jnp.mean(x * x, axis=-1, keepdims=True)
    return x * lax.rsqrt(ms + EPS) * g


def _sigmoid(x):
    return 1.0 / (1.0 + jnp.exp(-x))


def _rope128(x, cos_t, sin_t):
    lane = lax.broadcasted_iota(jnp.int32, x.shape, 1)
    first_half = (lane & (MLA_ROPE_DIM // 2)) == 0
    partner = jnp.where(first_half, pltpu.roll(x, LANES - 32, 1), pltpu.roll(x, 32, 1))
    return x * cos_t + partner * sin_t


def _softmax_parts(s):
    m = jnp.max(s, axis=-1, keepdims=True)
    e = jnp.exp2(s - m)
    return e, jnp.sum(e, axis=-1, keepdims=True)


def _mla_proj_kernel(x_ref, g_ref, w1_ref, gq_ref, gkv_ref, wq_ref, wkv_ref, cos_ref, sin_ref,
                     q_ref, k_ref, v_ref, xn_ref, *, q_rank, kv_rank, q_scale):
    xn = _rms(x_ref[...], g_ref[...]).astype(BF16)
    xn_ref[...] = xn
    z = jnp.dot(xn, w1_ref[...], preferred_element_type=F32)
    cqn = _rms(z[:, :q_rank], gq_ref[...]).astype(BF16)
    ckvn = _rms(z[:, q_rank:q_rank + kv_rank], gkv_ref[...]).astype(BF16)
    cos_t, sin_t = cos_ref[...], sin_ref[...]
    kpe = _rope128(z[:, q_rank + kv_rank:], cos_t, sin_t).astype(BF16)
    q = jnp.dot(cqn, wq_ref[...], preferred_element_type=F32) * q_scale
    kv = jnp.dot(ckvn, wkv_ref[...], preferred_element_type=F32)
    for h in range(MLA_HEADS):
        lo, mid, hi = h * MLA_QK_PAD, h * MLA_QK_PAD + MLA_NOPE_DIM, (h + 1) * MLA_QK_PAD
        q_ref[:, lo:mid] = q[:, lo:mid].astype(BF16)
        q_ref[:, mid:hi] = _rope128(q[:, mid:hi], cos_t, sin_t).astype(BF16)
        k_ref[:, lo:mid] = kv[:, h * MLA_NOPE_DIM:(h + 1) * MLA_NOPE_DIM].astype(BF16)
        k_ref[:, mid:hi] = kpe
    v_ref[...] = kv[:, MLA_HEADS * MLA_NOPE_DIM:].astype(BF16)


def _mla_proj(x, g, w1, gq, gkv, wq, wkv, cos_t, sin_t, *, tm):
    t, d = x.shape
    q_rank, kv_rank = gq.shape[1], gkv.shape[1]
    full = lambda a: pl.BlockSpec(a.shape, lambda i: (0, 0))
    rows = lambda width: pl.BlockSpec((tm, width), lambda i: (i, 0))
    qk_w, v_w = MLA_HEADS * MLA_QK_PAD, MLA_HEADS * MLA_V_DIM
    kern = functools.partial(_mla_proj_kernel, q_rank=q_rank, kv_rank=kv_rank,
                             q_scale=LOG2_E * (MLA_NOPE_DIM + MLA_ROPE_DIM) ** -0.5)
    return pl.pallas_call(
        kern,
        grid=(t // tm,),
        in_specs=[rows(d), full(g), full(w1), full(gq), full(gkv), full(wq), full(wkv),
                  rows(LANES), rows(LANES)],
        out_specs=[rows(qk_w), rows(qk_w), rows(v_w), rows(d)],
        out_shape=[jax.ShapeDtypeStruct((t, qk_w), BF16), jax.ShapeDtypeStruct((t, qk_w), BF16),
                   jax.ShapeDtypeStruct((t, v_w), BF16), jax.ShapeDtypeStruct((t, d), BF16)],
        compiler_params=_params("parallel"),
        name="mla_proj",
    )(x, g, w1, gq, gkv, wq, wkv, cos_t, sin_t)


def _stage_shifted_weights(wa_ref, wb_ref, wbf_ref, shift):
    n = wa_ref.shape[1]
    rows = wa_ref.shape[0]
    lane = lax.broadcasted_iota(jnp.int32, (rows, LANES), 1)
    take_cur = lane < LANES - shift

    def rolled(c):
        src = wa_ref[:, c:c + LANES] if c < n else wb_ref[...]
        return pltpu.roll(src, LANES - shift, 1)

    cur = rolled(0)
    for c in range(0, n, LANES):
        nxt = rolled(c + LANES)
        wbf_ref[:, c:c + LANES] = jnp.where(take_cur, cur, nxt).astype(BF16)
        cur = nxt


def _qkv_proj_kernel(xn_ref, wa_ref, wb_ref, cos_ref, sin_ref, o_ref, wbf_ref, *, dq_scale, shift):
    j = pl.program_id(0)
    sub = range(0, o_ref.shape[1], MXU_COLS)

    @pl.when(pl.program_id(1) == 0)
    def _():
        _stage_shifted_weights(wa_ref, wb_ref, wbf_ref, shift)

    def z_cols(c):
        return jnp.dot(xn_ref[...], wbf_ref[:, c:c + MXU_COLS], preferred_element_type=F32)

    def rope_out(scale):
        cos_t, sin_t = cos_ref[...], sin_ref[...]
        for c in sub:
            z = z_cols(c)
            for cc in range(0, MXU_COLS, LANES):
                zc = z[:, cc:cc + LANES]
                if scale != 1.0:
                    zc = zc * scale
                o_ref[:, c + cc:c + cc + LANES] = _rope128(zc, cos_t, sin_t).astype(BF16)

    @pl.when(j == 0)
    def _():
        rope_out(dq_scale)

    @pl.when(j == 1)
    def _():
        rope_out(1.0)

    @pl.when(j == 2)
    def _():
        for c in sub:
            o_ref[:, c:c + MXU_COLS] = z_cols(c).astype(BF16)


def _gate_proj_kernel(xn_ref, wa_ref, wb_ref, o_ref, wbf_ref, *, shift):
    @pl.when(pl.program_id(1) == 0)
    def _():
        _stage_shifted_weights(wa_ref, wb_ref, wbf_ref, shift)

    for c in range(0, o_ref.shape[1], MXU_COLS):
        z = jnp.dot(xn_ref[...], wbf_ref[:, c:c + MXU_COLS], preferred_element_type=F32)
        o_ref[:, c:c + MXU_COLS] = _sigmoid(z).astype(BF16)


def _shifted_weight_specs(layer, d, tn, first_col):
    base, shift = divmod(first_col, LANES)
    assert (base * LANES) % tn == 0 and 0 < shift < LANES
    jb = base * LANES // tn
    return ([pl.BlockSpec((None, d, tn), lambda j, i: (layer, 0, jb + j)),
             pl.BlockSpec((None, d, LANES), lambda j, i: (layer, 0, (jb + j + 1) * (tn // LANES)))], shift)


def _qkv_proj(xn, w_in, layer, cos_t, sin_t, *, tm, tn, first_col):
    t, d = xn.shape
    w_specs, shift = _shifted_weight_specs(layer, d, tn, first_col)
    kern = functools.partial(_qkv_proj_kernel, dq_scale=LOG2_E * DIFF_HEAD_DIM ** -0.5, shift=shift)
    return pl.pallas_call(
        kern,
        grid=(3, t // tm),
        in_specs=[pl.BlockSpec((tm, d), lambda j, i: (i, 0))] + w_specs
                 + [pl.BlockSpec((tm, LANES), lambda j, i: (i, 0)),
                    pl.BlockSpec((tm, LANES), lambda j, i: (i, 0))],
        out_specs=pl.BlockSpec((tm, tn), lambda j, i: (i, j)),
        out_shape=jax.ShapeDtypeStruct((t, 3 * tn), BF16),
        scratch_shapes=[pltpu.VMEM((d, tn), BF16)],
        compiler_params=_params("arbitrary", "arbitrary"),
        name="qkv_proj",
    )(xn, w_in, w_in, cos_t, sin_t)


def _gate_proj(xn, w_in, layer, *, tm, tn, first_col, n_blocks):
    t, d = xn.shape
    w_specs, shift = _shifted_weight_specs(layer, d, tn, first_col)
    return pl.pallas_call(
        functools.partial(_gate_proj_kernel, shift=shift),
        grid=(n_blocks, t // tm),
        in_specs=[pl.BlockSpec((tm, d), lambda j, i: (i, 0))] + w_specs,
        out_specs=pl.BlockSpec((tm, tn), lambda j, i: (i, j)),
        out_shape=jax.ShapeDtypeStruct((t, n_blocks * tn), BF16),
        scratch_shapes=[pltpu.VMEM((d, tn), BF16)],
        compiler_params=_params("arbitrary", "arbitrary"),
        name="gate_proj",
    )(xn, w_in, w_in)


def _rows(c, tq):
    if isinstance(c, int):
        return slice(c * tq, (c + 1) * tq)
    return pl.ds(pl.multiple_of(c * tq, tq), tq)


def _pipelined_chunks(n, qk, sm, pv):
    def step(c, parity):
        qk(c, parity)
        sm(1 - parity)
        pv(c - 2, parity)

    qk(0, 0)
    qk(1, 1)
    sm(0)

    def body(i, carry):
        step(2 * i, 0)
        step(2 * i + 1, 1)
        return carry

    lax.fori_loop(1, n // 2, body, 0)
    sm(1)
    pv(n - 2, 0)
    pv(n - 1, 1)


def _fill_v_ones(va_ref, v_ref):
    w = v_ref.shape[1]
    va_ref[:, :w] = v_ref[...]
    va_ref[:, w:] = jnp.ones((va_ref.shape[0], va_ref.shape[1] - w), BF16)


def _exp_scores(s):
    return jnp.exp2(s - jnp.max(s, axis=-1, keepdims=True)).astype(BF16)


def _mla_attn_kernel(q_ref, k_ref, v_ref, o_ref, s0, s1, p0, p1, va_ref, *, tq):
    sb, pb = (s0, s1), (p0, p1)
    dv = v_ref.shape[1]
    _fill_v_ones(va_ref, v_ref)

    def qk(c, slot):
        sb[slot][...] = lax.dot_general(q_ref[_rows(c, tq), :], k_ref[...], _NT, preferred_element_type=F32)

    def sm(slot):
        pb[slot][...] = _exp_scores(sb[slot][...])

    def pv(c, slot):
        o = jnp.dot(pb[slot][...], va_ref[...], preferred_element_type=F32)
        o_ref[_rows(c, tq), :] = (o[:, :dv] * (1.0 / o[:, dv:])).astype(BF16)

    _pipelined_chunks(q_ref.shape[0] // tq, qk, sm, pv)


def _mla_attn(q, k, v, *, batch, tq):
    t = q.shape[0]
    s = t // batch
    return pl.pallas_call(
        functools.partial(_mla_attn_kernel, tq=tq),
        grid=(batch, MLA_HEADS),
        in_specs=[pl.BlockSpec((s, MLA_QK_PAD), lambda b, h: (b, h)),
                  pl.BlockSpec((s, MLA_QK_PAD), lambda b, h: (b, h)),
                  pl.BlockSpec((s, MLA_V_DIM), lambda b, h: (b, h))],
        out_specs=pl.BlockSpec((s, MLA_V_DIM), lambda b, h: (b, h)),
        out_shape=jax.ShapeDtypeStruct((t, MLA_HEADS * MLA_V_DIM), BF16),
        scratch_shapes=[pltpu.VMEM((tq, s), F32), pltpu.VMEM((tq, s), F32),
                        pltpu.VMEM((tq, s), BF16), pltpu.VMEM((tq, s), BF16),
                        pltpu.VMEM((s, 2 * MLA_V_DIM), BF16)],
        compiler_params=_params("parallel", "parallel"),
        name="mla_attn",
    )(q, k, v)


def _diff_attn_kernel(lq1_ref, lk1_ref, lq2_ref, lk2_ref, gsub_ref, q_ref, k_ref, v_ref, o_ref,
                      s0, s1, p0, p1, va_ref, *, tq, lam_init):
    lam = (jnp.exp(jnp.sum(lq1_ref[...] * lk1_ref[...], axis=-1, keepdims=True))
           - jnp.exp(jnp.sum(lq2_ref[...] * lk2_ref[...], axis=-1, keepdims=True)) + lam_init)
    gain = gsub_ref[...] * (1.0 - lam_init)
    lane = lax.broadcasted_iota(jnp.int32, (tq, 2 * DIFF_HEAD_DIM), 1)
    map0 = lane < DIFF_HEAD_DIM
    sb, pb = (s0, s1), (p0, p1)
    dv = v_ref.shape[1]
    _fill_v_ones(va_ref, v_ref)

    def qk(c, slot):
        q = q_ref[_rows(c, tq), :]
        zero = jnp.zeros_like(q)
        q01 = jnp.concatenate([jnp.where(map0, q, zero), jnp.where(map0, zero, q)], axis=0)
        sb[slot][...] = lax.dot_general(q01, k_ref[...], _NT, preferred_element_type=F32)

    def sm(slot):
        pb[slot][...] = _exp_scores(sb[slot][...])

    def pv(c, slot):
        o = jnp.dot(pb[slot][...], va_ref[...], preferred_element_type=F32)
        o = o[:, :dv] * (1.0 / o[:, dv:])
        o_ref[_rows(c, tq), :] = _rms(o[:tq] - lam * o[tq:], gain).astype(BF16)

    _pipelined_chunks(q_ref.shape[0] // tq, qk, sm, pv)


def _diff_attn(lq1, lk1, lq2, lk2, gsub, z, *, batch, tq, q_col, k_col, v_col, lam_init):
    t = z.shape[0]
    s = t // batch
    hw = 2 * DIFF_HEAD_DIM
    small = lambda a: pl.BlockSpec(a.shape, lambda b, h: (0, 0))
    head = lambda col: pl.BlockSpec((s, hw), lambda b, h: (b, col + h))
    return pl.pallas_call(
        functools.partial(_diff_attn_kernel, tq=tq, lam_init=lam_init),
        grid=(batch, DIFF_HEADS),
        in_specs=[small(lq1), small(lk1), small(lq2), small(lk2), small(gsub),
                  head(q_col), head(k_col), head(v_col)],
        out_specs=pl.BlockSpec((s, DIFF_V_DIM), lambda b, h: (b, h)),
        out_shape=jax.ShapeDtypeStruct((t, DIFF_HEADS * DIFF_V_DIM), BF16),
        scratch_shapes=[pltpu.VMEM((2 * tq, s), F32), pltpu.VMEM((2 * tq, s), F32),
                        pltpu.VMEM((2 * tq, s), BF16), pltpu.VMEM((2 * tq, s), BF16),
                        pltpu.VMEM((s, 2 * DIFF_V_DIM), BF16)],
        compiler_params=_params("parallel", "parallel"),
        name="diff_attn",
    )(lq1, lk1, lq2, lk2, gsub, z, z, z)


def _mem_kv_kernel(m_ref, g_ref, w_ref, o_ref):
    mn = _rms(m_ref[...], g_ref[...]).astype(BF16)
    o_ref[...] = jnp.dot(mn, w_ref[...], preferred_element_type=F32).astype(BF16)


def _mem_kv(mem, g, w, *, tm):
    t, d = mem.shape
    n = w.shape[1]
    return pl.pallas_call(
        _mem_kv_kernel,
        grid=(t // tm,),
        in_specs=[pl.BlockSpec((tm, d), lambda i: (i, 0)),
                  pl.BlockSpec((1, d), lambda i: (0, 0)),
                  pl.BlockSpec((d, n), lambda i: (0, 0))],
        out_specs=pl.BlockSpec((tm, n), lambda i: (i, 0)),
        out_shape=jax.ShapeDtypeStruct((t, n), BF16),
        compiler_params=_params("parallel"),
        name="mem_kv",
    )(mem, g, w)


def _merge_cross_kernel(x_ref, oa_ref, ob_ref, gate_ref, woa_ref, wob_ref, wout_ref,
                        gc_ref, wcq_ref, mkv_ref, wco_ref, o_ref, *, q_scale):
    d = x_ref.shape[1]
    ya = jnp.dot(oa_ref[...], woa_ref[...], preferred_element_type=F32)
    yb = jnp.dot(ob_ref[...], wob_ref[...], preferred_element_type=F32)
    merged = gate_ref[:, :d].astype(F32) * ya + gate_ref[:, d:].astype(F32) * yb
    x1 = x_ref[...] + jnp.dot(merged.astype(BF16), wout_ref[...], preferred_element_type=F32)

    hn = _rms(x1, gc_ref[...]).astype(BF16)
    qc = (jnp.dot(hn, wcq_ref[...], preferred_element_type=F32) * q_scale).astype(BF16)
    hd = XATTN_HEAD_DIM
    heads = []
    for h in range(XATTN_HEADS):
        k_h = mkv_ref[:, h * hd:(h + 1) * hd]
        v_h = mkv_ref[:, (XATTN_HEADS + h) * hd:(XATTN_HEADS + h + 1) * hd]
        s = lax.dot_general(qc[:, h * hd:(h + 1) * hd], k_h, _NT, preferred_element_type=F32)
        e, l = _softmax_parts(s)
        o_h = jnp.dot(e.astype(BF16), v_h, preferred_element_type=F32) * (1.0 / l)
        heads.append(o_h.astype(BF16))
    oc = jnp.concatenate(heads, axis=1)
    o_ref[...] = x1 + jnp.dot(oc, wco_ref[...], preferred_element_type=F32)


def _merge_cross(x, oa, ob, gates, woa, wob, wout, gc, wcq, mkv, wco, *, tm, batch):
    t, d = x.shape
    blocks_per_seq = t // batch // tm
    m = mkv.shape[0] // batch
    const = lambda a: pl.BlockSpec(a.shape, lambda i: (0, 0), pipeline_mode=pl.Buffered(1))
    rows = lambda width: pl.BlockSpec((tm, width), lambda i: (i, 0))
    return pl.pallas_call(
        functools.partial(_merge_cross_kernel, q_scale=LOG2_E * XATTN_HEAD_DIM ** -0.5),
        grid=(t // tm,),
        in_specs=[rows(d), rows(oa.shape[1]), rows(ob.shape[1]), rows(2 * d),
                  const(woa), const(wob), const(wout), const(gc), const(wcq),
                  pl.BlockSpec((m, mkv.shape[1]), lambda i: (i // blocks_per_seq, 0)),
                  const(wco)],
        out_specs=rows(d),
        out_shape=jax.ShapeDtypeStruct((t, d), F32),
        compiler_params=_params("parallel"),
        name="merge_cross",
    )(x, oa, ob, gates, woa, wob, wout, gc, wcq, mkv, wco)


def _conv_ffn_kernel(xp_ref, x_ref, xn_ref, g_ref, wg_ref, wv_ref, cwg_ref, cwv_ref, cbg_ref, cbv_ref,
                     wd_ref, gfin_ref, o_ref, h_ref, ug_ref, uv_ref, acc_ref,
                     *, tm, halo, blocks_per_seq, final_norm):
    i, f = pl.program_id(0), pl.program_id(1)

    @pl.when(f == 0)
    def _():
        g = g_ref[...]
        pos = i % blocks_per_seq
        keep_prev = jnp.where(pos == 0, 0.0, 1.0)
        keep_next = jnp.where(pos == blocks_per_seq - 1, 0.0, 1.0)
        h_ref[0:halo, :] = (_rms(xp_ref[...], g) * keep_prev).astype(BF16)
        h_ref[halo:halo + tm, :] = _rms(x_ref[...], g).astype(BF16)
        h_ref[halo + tm:, :] = (_rms(xn_ref[...], g) * keep_next).astype(BF16)
        acc_ref[...] = jnp.zeros_like(acc_ref)

    h = h_ref[...]
    ug_ref[...] = jnp.dot(h, wg_ref[...], preferred_element_type=F32)
    uv_ref[...] = jnp.dot(h, wv_ref[...], preferred_element_type=F32)

    def conv(u_ref, cw_ref, cb_ref):
        cw = cw_ref[...]
        return (u_ref[halo - 1:halo - 1 + tm, :] * cw[0:1, :] + u_ref[halo:halo + tm, :] * cw[1:2, :]
                + u_ref[halo + 1:halo + 1 + tm, :] * cw[2:3, :] + cb_ref[...])

    yg = conv(ug_ref, cwg_ref, cbg_ref)
    yv = conv(uv_ref, cwv_ref, cbv_ref)
    act = (yg * _sigmoid(yg) * yv).astype(BF16)
    acc_ref[...] += jnp.dot(act, wd_ref[...], preferred_element_type=F32)

    @pl.when(f == pl.num_programs(1) - 1)
    def _():
        y = x_ref[...] + acc_ref[...]
        o_ref[...] = _rms(y, gfin_ref[...]) if final_norm else y


def _conv_ffn(x, g, w_up, conv_w, conv_b, w_down, g_final, *, tm, tf, batch, final_norm):
    t, d = x.shape
    ffn = w_down.shape[0]
    nf = ffn // tf
    halo = BF16_SUBLANES
    blocks_per_seq = t // batch // tm
    hb = tm // halo
    last_halo_block = t // halo - 1
    kern = functools.partial(_conv_ffn_kernel, tm=tm, halo=halo, blocks_per_seq=blocks_per_seq,
                             final_norm=final_norm)
    return pl.pallas_call(
        kern,
        grid=(t // tm, nf),
        in_specs=[pl.BlockSpec((halo, d), lambda i, f: (jnp.maximum(i * hb - 1, 0), 0)),
                  pl.BlockSpec((tm, d), lambda i, f: (i, 0)),
                  pl.BlockSpec((halo, d), lambda i, f: (jnp.minimum((i + 1) * hb, last_halo_block), 0)),
                  pl.BlockSpec((1, d), lambda i, f: (0, 0)),
                  pl.BlockSpec((d, tf), lambda i, f: (0, f)),
                  pl.BlockSpec((d, tf), lambda i, f: (0, nf + f)),
                  pl.BlockSpec((conv_w.shape[0], tf), lambda i, f: (0, f)),
                  pl.BlockSpec((conv_w.shape[0], tf), lambda i, f: (0, nf + f)),
                  pl.BlockSpec((1, tf), lambda i, f: (0, f)),
                  pl.BlockSpec((1, tf), lambda i, f: (0, nf + f)),
                  pl.BlockSpec((tf, d), lambda i, f: (f, 0)),
                  pl.BlockSpec((1, d), lambda i, f: (0, 0))],
        out_specs=pl.BlockSpec((tm, d), lambda i, f: (i, 0)),
        out_shape=jax.ShapeDtypeStruct((t, d), F32),
        scratch_shapes=[pltpu.VMEM((tm + 2 * halo, d), BF16),
                        pltpu.VMEM((tm + 2 * halo, tf), F32),
                        pltpu.VMEM((tm + 2 * halo, tf), F32),
                        pltpu.VMEM((tm, d), F32)],
        compiler_params=_params("parallel", "arbitrary"),
        name="conv_ffn",
    )(x, x, x, g, w_up, w_up, conv_w, conv_w, conv_b, conv_b, w_down, g_final)


def _rope_tables(positions):
    dim = MLA_ROPE_DIM
    inv = ROPE_THETA ** (-jnp.arange(0, dim, 2, dtype=F32) / dim)
    ang = positions.astype(F32)[..., None] * inv
    cos, sin = jnp.cos(ang), jnp.sin(ang)
    reps = LANES // dim
    cos_t = jnp.concatenate([cos, cos] * reps, axis=-1)
    sin_t = jnp.concatenate([-sin, sin] * reps, axis=-1)
    return cos_t.reshape(-1, LANES), sin_t.reshape(-1, LANES)


def _split_heads_kv(w, heads, dim):
    r = w.shape[0]
    return w.reshape(r, heads, 2, dim).transpose(0, 2, 1, 3).reshape(r, 2 * heads * dim)


def kernel(x, mem, positions, g_mix_norm, w_in, g_q_norm, w_uq, g_kv_norm, w_ukv, w_o_mla, lambda_q1, lambda_k1, lambda_q2, lambda_k2, g_diff_sub, w_o_diff, w_out, g_cross_norm, g_mem_norm, w_cross_q, w_cross_kv, w_cross_o, g_ffn_norm, w_up, conv_w, conv_b, w_down, g_final):
    batch, seq, d = x.shape
    depth = w_in.shape[0]
    t = batch * seq
    assert depth >= 1 and MLA_ROPE_DIM == DIFF_HEAD_DIM and MLA_NOPE_DIM == MLA_V_DIM
    q_rank, kv_rank = g_q_norm.shape[1], g_kv_norm.shape[1]
    dqk = DIFF_HEADS * 2 * DIFF_HEAD_DIM
    dv_w = DIFF_HEADS * DIFF_V_DIM
    c0 = q_rank + kv_rank + MLA_ROPE_DIM
    tn = 1024
    assert dqk == tn and dv_w == tn and (2 * d) % tn == 0

    cos_t, sin_t = _rope_tables(positions)
    xf = x.reshape(t, d)
    memf = mem.reshape(-1, d)
    row = lambda v: v.reshape(1, -1)

    for l in range(depth):
        lam_init = 0.8 - 0.6 * math.exp(-0.3 * l)
        wi = w_in[l]
        w1 = wi[:, :c0 + LANES - MLA_ROPE_DIM].astype(BF16)
        wq = jnp.pad(w_uq[l].reshape(q_rank, MLA_HEADS, MLA_NOPE_DIM + MLA_ROPE_DIM),
                     ((0, 0), (0, 0), (0, MLA_QK_PAD - MLA_NOPE_DIM - MLA_ROPE_DIM)))
        wq = wq.reshape(q_rank, MLA_HEADS * MLA_QK_PAD).astype(BF16)
        wkv = _split_heads_kv(w_ukv[l], MLA_HEADS, MLA_NOPE_DIM).astype(BF16)
        wckv = _split_heads_kv(w_cross_kv[l], XATTN_HEADS, XATTN_HEAD_DIM).astype(BF16)

        q, k, v, xn = _mla_proj(xf, row(g_mix_norm[l]), w1, row(g_q_norm[l]), row(g_kv_norm[l]), wq, wkv,
                                cos_t, sin_t, tm=512)
        z = _qkv_proj(xn, w_in, l, cos_t, sin_t, tm=1024, tn=tn, first_col=c0)
        gates = _gate_proj(xn, w_in, l, tm=1024, tn=tn, first_col=c0 + 3 * tn, n_blocks=2 * d // tn)
        oa = _mla_attn(q, k, v, batch=batch, tq=512)
        ob = _diff_attn(row(lambda_q1[l]), row(lambda_k1[l]), row(lambda_q2[l]), row(lambda_k2[l]),
                        row(g_diff_sub[l]), z, batch=batch, tq=512,
                        q_col=0, k_col=DIFF_HEADS, v_col=2 * DIFF_HEADS, lam_init=lam_init)
        mkv = _mem_kv(memf, row(g_mem_norm[l]), wckv, tm=memf.shape[0] // batch)
        xf = _merge_cross(xf, oa, ob, gates, w_o_mla[l].astype(BF16), w_o_diff[l].astype(BF16),
                          w_out[l].astype(BF16), row(g_cross_norm[l]), w_cross_q[l].astype(BF16), mkv,
                          w_cross_o[l].astype(BF16), tm=512, batch=batch)
        xf = _conv_ffn(xf, row(g_ffn_norm[l]), w_up[l].astype(BF16), conv_w[l], row(conv_b[l]),
                       w_down[l].astype(BF16), row(g_final), tm=512, tf=512, batch=batch,
                       final_norm=(l == depth - 1))
    return xf.reshape(batch, seq, d)
```

```python
import functools
import math

import jax
import jax.numpy as jnp
from jax import lax
from jax.experimental import pallas as pl
from jax.experimental.pallas import tpu as pltpu

F32 = jnp.float32
BF16 = jnp.bfloat16

EPS = 1e-6
LOG2_E = math.log2(math.e)
ROPE_THETA = 10000.0
MLA_HEADS = 8
MLA_NOPE_DIM = 128
MLA_ROPE_DIM = 64
MLA_V_DIM = 128
MLA_QK_PAD = 256
DIFF_HEADS = 8
DIFF_HEAD_DIM = 64
DIFF_V_DIM = 2 * DIFF_HEAD_DIM
XATTN_HEADS = 4
XATTN_HEAD_DIM = 128
LANES = 128
BF16_SUBLANES = 16
MXU_COLS = 256
VMEM_LIMIT_BYTES = 56 * 1024 * 1024

_NT = (((1,), (1,)), ((), ()))


def _params(*semantics):
    return pltpu.CompilerParams(dimension_semantics=semantics, vmem_limit_bytes=VMEM_LIMIT_BYTES)


def _rms(x, g):
    ms = jnp.mean(x * x, axis=-1, keepdims=True)
    return x * lax.rsqrt(ms + EPS) * g


def _sigmoid(x):
    return 1.0 / (1.0 + jnp.exp(-x))


def _rope128(x, cos_t, sin_t):
    lane = lax.broadcasted_iota(jnp.int32, x.shape, 1)
    first_half = (lane & (MLA_ROPE_DIM // 2)) == 0
    partner = jnp.where(first_half, pltpu.roll(x, LANES - 32, 1), pltpu.roll(x, 32, 1))
    return x * cos_t + partner * sin_t


def _softmax_parts(s):
    m = jnp.max(s, axis=-1, keepdims=True)
    e = jnp.exp2(s - m)
    return e, jnp.sum(e, axis=-1, keepdims=True)


def _mla_proj_kernel(x_ref, g_ref, w1t_ref, gq_ref, gkv_ref, wq_ref, wkv_ref, cos_ref, sin_ref,
                     q_ref, k_ref, v_ref, xn_ref, w1_ref, *, q_rank, kv_rank, q_scale):
    @pl.when(pl.program_id(0) == 0)
    def _():
        w1_ref[...] = w1t_ref[...].astype(BF16)

    xn = _rms(x_ref[...], g_ref[...]).astype(BF16)
    xn_ref[...] = xn
    z = lax.dot_general(xn, w1_ref[...], _NT, preferred_element_type=F32)
    cqn = _rms(z[:, :q_rank], gq_ref[...]).astype(BF16)
    ckvn = _rms(z[:, q_rank:q_rank + kv_rank], gkv_ref[...]).astype(BF16)
    cos_t, sin_t = cos_ref[...], sin_ref[...]
    kpe = _rope128(z[:, q_rank + kv_rank:], cos_t, sin_t).astype(BF16)
    q = jnp.dot(cqn, wq_ref[...], preferred_element_type=F32) * q_scale
    kv = jnp.dot(ckvn, wkv_ref[...], preferred_element_type=F32)
    for h in range(MLA_HEADS):
        lo, mid, hi = h * MLA_QK_PAD, h * MLA_QK_PAD + MLA_NOPE_DIM, (h + 1) * MLA_QK_PAD
        q_ref[:, lo:mid] = q[:, lo:mid].astype(BF16)
        q_ref[:, mid:hi] = _rope128(q[:, mid:hi], cos_t, sin_t).astype(BF16)
        k_ref[:, lo:mid] = kv[:, h * MLA_NOPE_DIM:(h + 1) * MLA_NOPE_DIM].astype(BF16)
        k_ref[:, mid:hi] = kpe
    v_ref[...] = kv[:, MLA_HEADS * MLA_NOPE_DIM:].astype(BF16)


def _mla_proj(x, g, w_t, layer, n1, gq, gkv, wq, wkv, cos_t, sin_t, *, tm):
    t, d = x.shape
    q_rank, kv_rank = gq.shape[1], gkv.shape[1]
    full = lambda a: pl.BlockSpec(a.shape, lambda i: (0, 0))
    rows = lambda width: pl.BlockSpec((tm, width), lambda i: (i, 0))
    qk_w, v_w = MLA_HEADS * MLA_QK_PAD, MLA_HEADS * MLA_V_DIM
    kern = functools.partial(_mla_proj_kernel, q_rank=q_rank, kv_rank=kv_rank,
                             q_scale=LOG2_E * (MLA_NOPE_DIM + MLA_ROPE_DIM) ** -0.5)
    return pl.pallas_call(
        kern,
        grid=(t // tm,),
        in_specs=[rows(d), full(g),
                  pl.BlockSpec((None, n1, d), lambda i: (layer, 0, 0), pipeline_mode=pl.Buffered(1)),
                  full(gq), full(gkv), full(wq), full(wkv), rows(LANES), rows(LANES)],
        out_specs=[rows(qk_w), rows(qk_w), rows(v_w), rows(d)],
        out_shape=[jax.ShapeDtypeStruct((t, qk_w), BF16), jax.ShapeDtypeStruct((t, qk_w), BF16),
                   jax.ShapeDtypeStruct((t, v_w), BF16), jax.ShapeDtypeStruct((t, d), BF16)],
        scratch_shapes=[pltpu.VMEM((n1, d), BF16)],
        compiler_params=_params("arbitrary"),
        name="mla_proj",
    )(x, g, w_t, gq, gkv, wq, wkv, cos_t, sin_t)


def _qkv_proj_kernel(xn_ref, wt_ref, cos_ref, sin_ref, o_ref, wbf_ref, *, dq_scale):
    j = pl.program_id(0)
    sub = range(0, o_ref.shape[1], MXU_COLS)

    @pl.when(pl.program_id(1) == 0)
    def _():
        wbf_ref[...] = wt_ref[0].astype(BF16)

    def z_cols(c):
        return lax.dot_general(xn_ref[...], wbf_ref[c:c + MXU_COLS, :], _NT, preferred_element_type=F32)

    def rope_out(scale):
        cos_t, sin_t = cos_ref[...], sin_ref[...]
        for c in sub:
            z = z_cols(c)
            for cc in range(0, MXU_COLS, LANES):
                zc = z[:, cc:cc + LANES]
                if scale != 1.0:
                    zc = zc * scale
                o_ref[:, c + cc:c + cc + LANES] = _rope128(zc, cos_t, sin_t).astype(BF16)

    @pl.when(j == 0)
    def _():
        rope_out(dq_scale)

    @pl.when(j == 1)
    def _():
        rope_out(1.0)

    @pl.when(j == 2)
    def _():
        for c in sub:
            o_ref[:, c:c + MXU_COLS] = z_cols(c).astype(BF16)


def _gate_proj_kernel(xn_ref, wt_ref, o_ref, wbf_ref):
    @pl.when(pl.program_id(1) == 0)
    def _():
        wbf_ref[...] = wt_ref[0].astype(BF16)

    for c in range(0, o_ref.shape[1], MXU_COLS):
        z = lax.dot_general(xn_ref[...], wbf_ref[c:c + MXU_COLS, :], _NT, preferred_element_type=F32)
        o_ref[:, c:c + MXU_COLS] = _sigmoid(z).astype(BF16)


def _weight_rows_spec(layer, d, tn, first_row):
    return pl.BlockSpec((pl.Element(1), pl.Element(tn), pl.Element(d)),
                        lambda j, i: (layer, pl.multiple_of(first_row + j * tn, BF16_SUBLANES), 0))


def _qkv_proj(xn, w_t, layer, cos_t, sin_t, *, tm, tn, first_row):
    t, d = xn.shape
    kern = functools.partial(_qkv_proj_kernel, dq_scale=LOG2_E * DIFF_HEAD_DIM ** -0.5)
    return pl.pallas_call(
        kern,
        grid=(3, t // tm),
        in_specs=[pl.BlockSpec((tm, d), lambda j, i: (i, 0)),
                  _weight_rows_spec(layer, d, tn, first_row),
                  pl.BlockSpec((tm, LANES), lambda j, i: (i, 0)),
                  pl.BlockSpec((tm, LANES), lambda j, i: (i, 0))],
        out_specs=pl.BlockSpec((tm, tn), lambda j, i: (i, j)),
        out_shape=jax.ShapeDtypeStruct((t, 3 * tn), BF16),
        scratch_shapes=[pltpu.VMEM((tn, d), BF16)],
        compiler_params=_params("arbitrary", "arbitrary"),
        name="qkv_proj",
    )(xn, w_t, cos_t, sin_t)


def _gate_proj(xn, w_t, layer, *, tm, tn, first_row, n_blocks):
    t, d = xn.shape
    return pl.pallas_call(
        _gate_proj_kernel,
        grid=(n_blocks, t // tm),
        in_specs=[pl.BlockSpec((tm, d), lambda j, i: (i, 0)),
                  _weight_rows_spec(layer, d, tn, first_row)],
        out_specs=pl.BlockSpec((tm, tn), lambda j, i: (i, j)),
        out_shape=jax.ShapeDtypeStruct((t, n_blocks * tn), BF16),
        scratch_shapes=[pltpu.VMEM((tn, d), BF16)],
        compiler_params=_params("arbitrary", "arbitrary"),
        name="gate_proj",
    )(xn, w_t)


def _cast_specs(weights, layer, steps, step_index):
    ins, outs, shapes = [], [], []
    for w in weights:
        _, r, c = w.shape
        rb = r // steps
        assert rb * steps == r and rb % BF16_SUBLANES == 0
        ins.append(pl.BlockSpec((None, rb, c), lambda *g: (layer, step_index(*g), 0)))
        outs.append(pl.BlockSpec((rb, c), lambda *g: (step_index(*g), 0)))
        shapes.append(jax.ShapeDtypeStruct((r, c), BF16))
    return ins, outs, shapes


def _run_casts(src_refs, dst_refs):
    for src, dst in zip(src_refs, dst_refs):
        dst[...] = src[...].astype(BF16)


def _rows(c, tq):
    if isinstance(c, int):
        return slice(c * tq, (c + 1) * tq)
    return pl.ds(pl.multiple_of(c * tq, tq), tq)


def _pipelined_chunks(n, qk, sm, pv):
    def step(c, parity):
        qk(c, parity)
        sm(1 - parity)
        pv(c - 2, parity)

    qk(0, 0)
    qk(1, 1)
    sm(0)

    def body(i, carry):
        step(2 * i, 0)
        step(2 * i + 1, 1)
        return carry

    lax.fori_loop(1, n // 2, body, 0)
    sm(1)
    pv(n - 2, 0)
    pv(n - 1, 1)


def _fill_v_ones(va_ref, v_ref):
    w = v_ref.shape[1]
    va_ref[:, :w] = v_ref[...]
    va_ref[:, w:] = jnp.ones((va_ref.shape[0], va_ref.shape[1] - w), BF16)


def _exp_scores(s):
    return jnp.exp2(s - jnp.max(s, axis=-1, keepdims=True)).astype(BF16)


def _mla_attn_kernel(q_ref, k_ref, v_ref, *refs, tq, n_cast):
    o_ref = refs[n_cast]
    s0, s1, p0, p1, va_ref = refs[2 * n_cast + 1:]
    _run_casts(refs[:n_cast], refs[n_cast + 1:2 * n_cast + 1])
    sb, pb = (s0, s1), (p0, p1)
    dv = v_ref.shape[1]
    _fill_v_ones(va_ref, v_ref)

    def qk(c, slot):
        sb[slot][...] = lax.dot_general(q_ref[_rows(c, tq), :], k_ref[...], _NT, preferred_element_type=F32)

    def sm(slot):
        pb[slot][...] = _exp_scores(sb[slot][...])

    def pv(c, slot):
        o = jnp.dot(pb[slot][...], va_ref[...], preferred_element_type=F32)
        o_ref[_rows(c, tq), :] = (o[:, :dv] * (1.0 / o[:, dv:])).astype(BF16)

    _pipelined_chunks(q_ref.shape[0] // tq, qk, sm, pv)


def _mla_attn(q, k, v, cast_weights, layer, *, batch, tq):
    t = q.shape[0]
    s = t // batch
    c_in, c_out, c_shapes = _cast_specs(cast_weights, layer, batch * MLA_HEADS, lambda b, h: b * MLA_HEADS + h)
    return pl.pallas_call(
        functools.partial(_mla_attn_kernel, tq=tq, n_cast=len(cast_weights)),
        grid=(batch, MLA_HEADS),
        in_specs=[pl.BlockSpec((s, MLA_QK_PAD), lambda b, h: (b, h)),
                  pl.BlockSpec((s, MLA_QK_PAD), lambda b, h: (b, h)),
                  pl.BlockSpec((s, MLA_V_DIM), lambda b, h: (b, h))] + c_in,
        out_specs=[pl.BlockSpec((s, MLA_V_DIM), lambda b, h: (b, h))] + c_out,
        out_shape=[jax.ShapeDtypeStruct((t, MLA_HEADS * MLA_V_DIM), BF16)] + c_shapes,
        scratch_shapes=[pltpu.VMEM((tq, s), F32), pltpu.VMEM((tq, s), F32),
                        pltpu.VMEM((tq, s), BF16), pltpu.VMEM((tq, s), BF16),
                        pltpu.VMEM((s, 2 * MLA_V_DIM), BF16)],
        compiler_params=_params("parallel", "parallel"),
        name="mla_attn",
    )(q, k, v, *cast_weights)


def _diff_attn_kernel(lq1_ref, lk1_ref, lq2_ref, lk2_ref, gsub_ref, q_ref, k_ref, v_ref, *refs,
                      tq, lam_init, n_cast):
    o_ref = refs[n_cast]
    s0, s1, p0, p1, va_ref = refs[2 * n_cast + 1:]
    _run_casts(refs[:n_cast], refs[n_cast + 1:2 * n_cast + 1])
    lam = (jnp.exp(jnp.sum(lq1_ref[...] * lk1_ref[...], axis=-1, keepdims=True))
           - jnp.exp(jnp.sum(lq2_ref[...] * lk2_ref[...], axis=-1, keepdims=True)) + lam_init)
    gain = gsub_ref[...] * (1.0 - lam_init)
    lane = lax.broadcasted_iota(jnp.int32, (tq, 2 * DIFF_HEAD_DIM), 1)
    map0 = lane < DIFF_HEAD_DIM
    sb, pb = (s0, s1), (p0, p1)
    dv = v_ref.shape[1]
    _fill_v_ones(va_ref, v_ref)

    def qk(c, slot):
        q = q_ref[_rows(c, tq), :]
        zero = jnp.zeros_like(q)
        q01 = jnp.concatenate([jnp.where(map0, q, zero), jnp.where(map0, zero, q)], axis=0)
        sb[slot][...] = lax.dot_general(q01, k_ref[...], _NT, preferred_element_type=F32)

    def sm(slot):
        pb[slot][...] = _exp_scores(sb[slot][...])

    def pv(c, slot):
        o = jnp.dot(pb[slot][...], va_ref[...], preferred_element_type=F32)
        o = o[:, :dv] * (1.0 / o[:, dv:])
        o_ref[_rows(c, tq), :] = _rms(o[:tq] - lam * o[tq:], gain).astype(BF16)

    _pipelined_chunks(q_ref.shape[0] // tq, qk, sm, pv)


def _diff_attn(lq1, lk1, lq2, lk2, gsub, z, cast_weights, layer, *, batch, tq, q_col, k_col, v_col, lam_init):
    t = z.shape[0]
    s = t // batch
    hw = 2 * DIFF_HEAD_DIM
    small = lambda a: pl.BlockSpec(a.shape, lambda b, h: (0, 0))
    head = lambda col: pl.BlockSpec((s, hw), lambda b, h: (b, col + h))
    c_in, c_out, c_shapes = _cast_specs(cast_weights, layer, batch * DIFF_HEADS, lambda b, h: b * DIFF_HEADS + h)
    return pl.pallas_call(
        functools.partial(_diff_attn_kernel, tq=tq, lam_init=lam_init, n_cast=len(cast_weights)),
        grid=(batch, DIFF_HEADS),
        in_specs=[small(lq1), small(lk1), small(lq2), small(lk2), small(gsub),
                  head(q_col), head(k_col), head(v_col)] + c_in,
        out_specs=[pl.BlockSpec((s, DIFF_V_DIM), lambda b, h: (b, h))] + c_out,
        out_shape=[jax.ShapeDtypeStruct((t, DIFF_HEADS * DIFF_V_DIM), BF16)] + c_shapes,
        scratch_shapes=[pltpu.VMEM((2 * tq, s), F32), pltpu.VMEM((2 * tq, s), F32),
                        pltpu.VMEM((2 * tq, s), BF16), pltpu.VMEM((2 * tq, s), BF16),
                        pltpu.VMEM((s, 2 * DIFF_V_DIM), BF16)],
        compiler_params=_params("parallel", "parallel"),
        name="diff_attn",
    )(lq1, lk1, lq2, lk2, gsub, z, z, z, *cast_weights)


def _mem_kv_kernel(m_ref, g_ref, w_ref, o_ref):
    mn = _rms(m_ref[...], g_ref[...]).astype(BF16)
    o_ref[...] = jnp.dot(mn, w_ref[...], preferred_element_type=F32).astype(BF16)


def _mem_kv(mem, g, w, *, tm):
    t, d = mem.shape
    n = w.shape[1]
    return pl.pallas_call(
        _mem_kv_kernel,
        grid=(t // tm,),
        in_specs=[pl.BlockSpec((tm, d), lambda i: (i, 0)),
                  pl.BlockSpec((1, d), lambda i: (0, 0)),
                  pl.BlockSpec((d, n), lambda i: (0, 0))],
        out_specs=pl.BlockSpec((tm, n), lambda i: (i, 0)),
        out_shape=jax.ShapeDtypeStruct((t, n), BF16),
        compiler_params=_params("parallel"),
        name="mem_kv",
    )(mem, g, w)


def _merge_cross_kernel(x_ref, oa_ref, ob_ref, gate_ref, woa_ref, wob_ref, wout_ref,
                        gc_ref, wcq_ref, mkv_ref, wco_ref, o_ref, *, q_scale):
    d = x_ref.shape[1]
    ya = jnp.dot(oa_ref[...], woa_ref[...], preferred_element_type=F32)
    yb = jnp.dot(ob_ref[...], wob_ref[...], preferred_element_type=F32)
    merged = gate_ref[:, :d].astype(F32) * ya + gate_ref[:, d:].astype(F32) * yb
    x1 = x_ref[...] + jnp.dot(merged.astype(BF16), wout_ref[...], preferred_element_type=F32)

    hn = _rms(x1, gc_ref[...]).astype(BF16)
    qc = (jnp.dot(hn, wcq_ref[...], preferred_element_type=F32) * q_scale).astype(BF16)
    hd = XATTN_HEAD_DIM
    heads = []
    for h in range(XATTN_HEADS):
        k_h = mkv_ref[:, h * hd:(h + 1) * hd]
        v_h = mkv_ref[:, (XATTN_HEADS + h) * hd:(XATTN_HEADS + h + 1) * hd]
        s = lax.dot_general(qc[:, h * hd:(h + 1) * hd], k_h, _NT, preferred_element_type=F32)
        e, l = _softmax_parts(s)
        o_h = jnp.dot(e.astype(BF16), v_h, preferred_element_type=F32) * (1.0 / l)
        heads.append(o_h.astype(BF16))
    oc = jnp.concatenate(heads, axis=1)
    o_ref[...] = x1 + jnp.dot(oc, wco_ref[...], preferred_element_type=F32)


def _merge_cross(x, oa, ob, gates, woa, wob, wout, gc, wcq, mkv, wco, *, tm, batch):
    t, d = x.shape
    blocks_per_seq = t // batch // tm
    m = mkv.shape[0] // batch
    const = lambda a: pl.BlockSpec(a.shape, lambda i: (0, 0), pipeline_mode=pl.Buffered(1))
    rows = lambda width: pl.BlockSpec((tm, width), lambda i: (i, 0))
    return pl.pallas_call(
        functools.partial(_merge_cross_kernel, q_scale=LOG2_E * XATTN_HEAD_DIM ** -0.5),
        grid=(t // tm,),
        in_specs=[rows(d), rows(oa.shape[1]), rows(ob.shape[1]), rows(2 * d),
                  const(woa), const(wob), const(wout), const(gc), const(wcq),
                  pl.BlockSpec((m, mkv.shape[1]), lambda i: (i // blocks_per_seq, 0)),
                  const(wco)],
        out_specs=rows(d),
        out_shape=jax.ShapeDtypeStruct((t, d), F32),
        compiler_params=_params("parallel"),
        name="merge_cross",
    )(x, oa, ob, gates, woa, wob, wout, gc, wcq, mkv, wco)


def _conv_ffn_kernel(xp_ref, x_ref, xn_ref, g_ref, wg_ref, wv_ref, cwg_ref, cwv_ref, cbg_ref, cbv_ref,
                     wd_ref, gfin_ref, o_ref, h_ref, ug_ref, uv_ref, acc_ref,
                     *, tm, halo, blocks_per_seq, final_norm):
    i, f = pl.program_id(0), pl.program_id(1)

    @pl.when(f == 0)
    def _():
        g = g_ref[...]
        pos = i % blocks_per_seq
        keep_prev = jnp.where(pos == 0, 0.0, 1.0)
        keep_next = jnp.where(pos == blocks_per_seq - 1, 0.0, 1.0)
        h_ref[0:halo, :] = (_rms(xp_ref[...], g) * keep_prev).astype(BF16)
        h_ref[halo:halo + tm, :] = _rms(x_ref[...], g).astype(BF16)
        h_ref[halo + tm:, :] = (_rms(xn_ref[...], g) * keep_next).astype(BF16)
        acc_ref[...] = jnp.zeros_like(acc_ref)

    h = h_ref[...]
    ug_ref[...] = jnp.dot(h, wg_ref[...], preferred_element_type=F32)
    uv_ref[...] = jnp.dot(h, wv_ref[...], preferred_element_type=F32)

    def conv(u_ref, cw_ref, cb_ref):
        cw = cw_ref[...]
        return (u_ref[halo - 1:halo - 1 + tm, :] * cw[0:1, :] + u_ref[halo:halo + tm, :] * cw[1:2, :]
                + u_ref[halo + 1:halo + 1 + tm, :] * cw[2:3, :] + cb_ref[...])

    yg = conv(ug_ref, cwg_ref, cbg_ref)
    yv = conv(uv_ref, cwv_ref, cbv_ref)
    act = (yg * _sigmoid(yg) * yv).astype(BF16)
    acc_ref[...] += jnp.dot(act, wd_ref[...], preferred_element_type=F32)

    @pl.when(f == pl.num_programs(1) - 1)
    def _():
        y = x_ref[...] + acc_ref[...]
        o_ref[...] = _rms(y, gfin_ref[...]) if final_norm else y


def _conv_ffn(x, g, w_up, conv_w, conv_b, w_down, g_final, *, tm, tf, batch, final_norm):
    t, d = x.shape
    ffn = w_down.shape[0]
    nf = ffn // tf
    halo = BF16_SUBLANES
    blocks_per_seq = t // batch // tm
    hb = tm // halo
    last_halo_block = t // halo - 1
    kern = functools.partial(_conv_ffn_kernel, tm=tm, halo=halo, blocks_per_seq=blocks_per_seq,
                             final_norm=final_norm)
    return pl.pallas_call(
        kern,
        grid=(t // tm, nf),
        in_specs=[pl.BlockSpec((halo, d), lambda i, f: (jnp.maximum(i * hb - 1, 0), 0)),
                  pl.BlockSpec((tm, d), lambda i, f: (i, 0)),
                  pl.BlockSpec((halo, d), lambda i, f: (jnp.minimum((i + 1) * hb, last_halo_block), 0)),
                  pl.BlockSpec((1, d), lambda i, f: (0, 0)),
                  pl.BlockSpec((d, tf), lambda i, f: (0, f)),
                  pl.BlockSpec((d, tf), lambda i, f: (0, nf + f)),
                  pl.BlockSpec((conv_w.shape[0], tf), lambda i, f: (0, f)),
                  pl.BlockSpec((conv_w.shape[0], tf), lambda i, f: (0, nf + f)),
                  pl.BlockSpec((1, tf), lambda i, f: (0, f)),
                  pl.BlockSpec((1, tf), lambda i, f: (0, nf + f)),
                  pl.BlockSpec((tf, d), lambda i, f: (f, 0)),
                  pl.BlockSpec((1, d), lambda i, f: (0, 0))],
        out_specs=pl.BlockSpec((tm, d), lambda i, f: (i, 0)),
        out_shape=jax.ShapeDtypeStruct((t, d), F32),
        scratch_shapes=[pltpu.VMEM((tm + 2 * halo, d), BF16),
                        pltpu.VMEM((tm + 2 * halo, tf), F32),
                        pltpu.VMEM((tm + 2 * halo, tf), F32),
                        pltpu.VMEM((tm, d), F32)],
        compiler_params=_params("parallel", "arbitrary"),
        name="conv_ffn",
    )(x, x, x, g, w_up, w_up, conv_w, conv_w, conv_b, conv_b, w_down, g_final)


def _rope_tables(positions):
    dim = MLA_ROPE_DIM
    inv = ROPE_THETA ** (-jnp.arange(0, dim, 2, dtype=F32) / dim)
    ang = positions.astype(F32)[..., None] * inv
    cos, sin = jnp.cos(ang), jnp.sin(ang)
    reps = LANES // dim
    cos_t = jnp.concatenate([cos, cos] * reps, axis=-1)
    sin_t = jnp.concatenate([-sin, sin] * reps, axis=-1)
    return cos_t.reshape(-1, LANES), sin_t.reshape(-1, LANES)


def _split_heads_kv(w, heads, dim):
    r = w.shape[0]
    return w.reshape(r, heads, 2, dim).transpose(0, 2, 1, 3).reshape(r, 2 * heads * dim)


def kernel(x, mem, positions, g_mix_norm, w_in, g_q_norm, w_uq, g_kv_norm, w_ukv, w_o_mla, lambda_q1, lambda_k1, lambda_q2, lambda_k2, g_diff_sub, w_o_diff, w_out, g_cross_norm, g_mem_norm, w_cross_q, w_cross_kv, w_cross_o, g_ffn_norm, w_up, conv_w, conv_b, w_down, g_final):
    batch, seq, d = x.shape
    depth = w_in.shape[0]
    t = batch * seq
    assert depth >= 1 and MLA_ROPE_DIM == DIFF_HEAD_DIM and MLA_NOPE_DIM == MLA_V_DIM
    q_rank, kv_rank = g_q_norm.shape[1], g_kv_norm.shape[1]
    dqk = DIFF_HEADS * 2 * DIFF_HEAD_DIM
    dv_w = DIFF_HEADS * DIFF_V_DIM
    c0 = q_rank + kv_rank + MLA_ROPE_DIM
    tn = 1024
    assert dqk == tn and dv_w == tn and (2 * d) % tn == 0

    cos_t, sin_t = _rope_tables(positions)
    xf = x.reshape(t, d)
    memf = mem.reshape(-1, d)
    row = lambda v: v.reshape(1, -1)
    w_in_t = jnp.swapaxes(w_in, 1, 2)

    for l in range(depth):
        lam_init = 0.8 - 0.6 * math.exp(-0.3 * l)
        wq = jnp.pad(w_uq[l].reshape(q_rank, MLA_HEADS, MLA_NOPE_DIM + MLA_ROPE_DIM),
                     ((0, 0), (0, 0), (0, MLA_QK_PAD - MLA_NOPE_DIM - MLA_ROPE_DIM)))
        wq = wq.reshape(q_rank, MLA_HEADS * MLA_QK_PAD).astype(BF16)
        wkv = _split_heads_kv(w_ukv[l], MLA_HEADS, MLA_NOPE_DIM).astype(BF16)
        wckv = _split_heads_kv(w_cross_kv[l], XATTN_HEADS, XATTN_HEAD_DIM).astype(BF16)

        q, k, v, xn = _mla_proj(xf, row(g_mix_norm[l]), w_in_t, l, c0 + LANES - MLA_ROPE_DIM,
                                row(g_q_norm[l]), row(g_kv_norm[l]), wq, wkv, cos_t, sin_t, tm=512)
        z = _qkv_proj(xn, w_in_t, l, cos_t, sin_t, tm=1024, tn=tn, first_row=c0)
        gates = _gate_proj(xn, w_in_t, l, tm=1024, tn=tn, first_row=c0 + 3 * tn, n_blocks=2 * d // tn)
        oa, w_up_b, w_down_b = _mla_attn(q, k, v, [w_up, w_down], l, batch=batch, tq=512)
        ob, w_oa_b, w_ob_b, w_out_b, w_cq_b, w_co_b = _diff_attn(
            row(lambda_q1[l]), row(lambda_k1[l]), row(lambda_q2[l]), row(lambda_k2[l]), row(g_diff_sub[l]), z,
            [w_o_mla, w_o_diff, w_out, w_cross_q, w_cross_o], l, batch=batch, tq=512,
            q_col=0, k_col=DIFF_HEADS, v_col=2 * DIFF_HEADS, lam_init=lam_init)
        mkv = _mem_kv(memf, row(g_mem_norm[l]), wckv, tm=memf.shape[0] // batch)
        xf = _merge_cross(xf, oa, ob, gates, w_oa_b, w_ob_b, w_out_b, row(g_cross_norm[l]), w_cq_b, mkv,
                          w_co_b, tm=512, batch=batch)
        xf = _conv_ffn(xf, row(g_ffn_norm[l]), w_up_b, conv_w[l], row(conv_b[l]),
                       w_down_b, row(g_final), tm=512, tf=512, batch=batch,
                       final_norm=(l == depth - 1))
    return xf.reshape(batch, seq, d)
```

```python
import functools
import math

import jax
import jax.numpy as jnp
from jax import lax
from jax.experimental import pallas as pl
from jax.experimental.pallas import tpu as pltpu

F32 = jnp.float32
BF16 = jnp.bfloat16

EPS = 1e-6
LOG2_E = math.log2(math.e)
ROPE_THETA = 10000.0
MLA_HEADS = 8
MLA_NOPE_DIM = 128
MLA_ROPE_DIM = 64
MLA_V_DIM = 128
MLA_QK_PAD = 256
DIFF_HEADS = 8
DIFF_HEAD_DIM = 64
DIFF_V_DIM = 2 * DIFF_HEAD_DIM
XATTN_HEADS = 4
XATTN_HEAD_DIM = 128
LANES = 128
BF16_SUBLANES = 16
MXU_COLS = 256
VMEM_LIMIT_BYTES = 56 * 1024 * 1024

_NT = (((1,), (1,)), ((), ()))


def _params(*semantics):
    return pltpu.CompilerParams(dimension_semantics=semantics, vmem_limit_bytes=VMEM_LIMIT_BYTES)


def _rms(x, g):
    ms = jnp.mean(x * x, axis=-1, keepdims=True)
    return x * lax.rsqrt(ms + EPS) * g


def _sigmoid(x):
    return 1.0 / (1.0 + jnp.exp(-x))


def _rope128(x, cos_t, sin_t):
    lane = lax.broadcasted_iota(jnp.int32, x.shape, 1)
    first_half = (lane & (MLA_ROPE_DIM // 2)) == 0
    partner = jnp.where(first_half, pltpu.roll(x, LANES - 32, 1), pltpu.roll(x, 32, 1))
    return x * cos_t + partner * sin_t


def _softmax_parts(s):
    m = jnp.max(s, axis=-1, keepdims=True)
    e = jnp.exp2(s - m)
    return e, jnp.sum(e, axis=-1, keepdims=True)


def _mla_proj_kernel(x_ref, g_ref, w1t_ref, gq_ref, gkv_ref, wq_ref, wkv_ref, cos_ref, sin_ref,
                     q_ref, k_ref, v_ref, xn_ref, w1_ref, *, q_rank, kv_rank, q_scale):
    @pl.when(pl.program_id(0) == 0)
    def _():
        w1_ref[...] = w1t_ref[...].astype(BF16)

    xn = _rms(x_ref[...], g_ref[...]).astype(BF16)
    xn_ref[...] = xn
    z = lax.dot_general(xn, w1_ref[...], _NT, preferred_element_type=F32)
    cqn = _rms(z[:, :q_rank], gq_ref[...]).astype(BF16)
    ckvn = _rms(z[:, q_rank:q_rank + kv_rank], gkv_ref[...]).astype(BF16)
    cos_t, sin_t = cos_ref[...], sin_ref[...]
    kpe = _rope128(z[:, q_rank + kv_rank:], cos_t, sin_t).astype(BF16)
    q = jnp.dot(cqn, wq_ref[...], preferred_element_type=F32) * q_scale
    kv = jnp.dot(ckvn, wkv_ref[...], preferred_element_type=F32)
    kv_w = MLA_NOPE_DIM + MLA_V_DIM
    for h in range(MLA_HEADS):
        lo, mid, hi = h * MLA_QK_PAD, h * MLA_QK_PAD + MLA_NOPE_DIM, (h + 1) * MLA_QK_PAD
        q_ref[:, lo:mid] = q[:, lo:mid].astype(BF16)
        q_ref[:, mid:hi] = _rope128(q[:, mid:hi], cos_t, sin_t).astype(BF16)
        k_ref[:, lo:mid] = kv[:, kv_w * h:kv_w * h + MLA_NOPE_DIM].astype(BF16)
        k_ref[:, mid:hi] = kpe
        v_ref[:, h * MLA_V_DIM:(h + 1) * MLA_V_DIM] = kv[:, kv_w * h + MLA_NOPE_DIM:kv_w * (h + 1)].astype(BF16)


def _mla_proj(x, g, w_t, layer, n1, gq, gkv, wq, wkv, cos_t, sin_t, *, tm):
    t, d = x.shape
    q_rank, kv_rank = gq.shape[1], gkv.shape[1]
    full = lambda a: pl.BlockSpec(a.shape, lambda i: (0, 0))
    rows = lambda width: pl.BlockSpec((tm, width), lambda i: (i, 0))
    qk_w, v_w = MLA_HEADS * MLA_QK_PAD, MLA_HEADS * MLA_V_DIM
    kern = functools.partial(_mla_proj_kernel, q_rank=q_rank, kv_rank=kv_rank,
                             q_scale=LOG2_E * (MLA_NOPE_DIM + MLA_ROPE_DIM) ** -0.5)
    return pl.pallas_call(
        kern,
        grid=(t // tm,),
        in_specs=[rows(d), full(g),
                  pl.BlockSpec((None, n1, d), lambda i: (layer, 0, 0), pipeline_mode=pl.Buffered(1)),
                  full(gq), full(gkv), full(wq), full(wkv), rows(LANES), rows(LANES)],
        out_specs=[rows(qk_w), rows(qk_w), rows(v_w), rows(d)],
        out_shape=[jax.ShapeDtypeStruct((t, qk_w), BF16), jax.ShapeDtypeStruct((t, qk_w), BF16),
                   jax.ShapeDtypeStruct((t, v_w), BF16), jax.ShapeDtypeStruct((t, d), BF16)],
        scratch_shapes=[pltpu.VMEM((n1, d), BF16)],
        compiler_params=_params("arbitrary"),
        name="mla_proj",
    )(x, g, w_t, gq, gkv, wq, wkv, cos_t, sin_t)


def _in_proj_kernel(xn_ref, wt_ref, o_ref, wbf_ref):
    @pl.when(pl.program_id(1) == 0)
    def _():
        wbf_ref[...] = wt_ref[0].astype(BF16)

    for c in range(0, o_ref.shape[1], MXU_COLS):
        z = lax.dot_general(xn_ref[...], wbf_ref[c:c + MXU_COLS, :], _NT, preferred_element_type=F32)
        o_ref[:, c:c + MXU_COLS] = z.astype(BF16)


def _in_proj(xn, w_t, layer, *, tm, tn, first_row, n_blocks):
    t, d = xn.shape
    w_spec = pl.BlockSpec((pl.Element(1), pl.Element(tn), pl.Element(d)),
                          lambda j, i: (layer, pl.multiple_of(first_row + j * tn, BF16_SUBLANES), 0))
    return pl.pallas_call(
        _in_proj_kernel,
        grid=(n_blocks, t // tm),
        in_specs=[pl.BlockSpec((tm, d), lambda j, i: (i, 0)), w_spec],
        out_specs=pl.BlockSpec((tm, tn), lambda j, i: (i, j)),
        out_shape=jax.ShapeDtypeStruct((t, n_blocks * tn), BF16),
        scratch_shapes=[pltpu.VMEM((tn, d), BF16)],
        compiler_params=_params("arbitrary", "arbitrary"),
        name="in_proj",
    )(xn, w_t)


def _cast_specs(weights, layer, steps, step_index):
    ins, outs, shapes = [], [], []
    for w in weights:
        _, r, c = w.shape
        rb = r // steps
        assert rb * steps == r and rb % BF16_SUBLANES == 0
        ins.append(pl.BlockSpec((None, rb, c), lambda *g: (layer, step_index(*g), 0)))
        outs.append(pl.BlockSpec((rb, c), lambda *g: (step_index(*g), 0)))
        shapes.append(jax.ShapeDtypeStruct((r, c), BF16))
    return ins, outs, shapes


def _run_casts(src_refs, dst_refs):
    for src, dst in zip(src_refs, dst_refs):
        dst[...] = src[...].astype(BF16)


def _rows(c, tq):
    if isinstance(c, int):
        return slice(c * tq, (c + 1) * tq)
    return pl.ds(pl.multiple_of(c * tq, tq), tq)


def _pipelined_chunks(n, qk, sm, pv):
    def step(c, parity):
        qk(c, parity)
        sm(1 - parity)
        pv(c - 2, parity)

    qk(0, 0)
    qk(1, 1)
    sm(0)

    def body(i, carry):
        step(2 * i, 0)
        step(2 * i + 1, 1)
        return carry

    lax.fori_loop(1, n // 2, body, 0)
    sm(1)
    pv(n - 2, 0)
    pv(n - 1, 1)


def _fill_v_ones(va_ref, v_ref):
    w = v_ref.shape[1]
    va_ref[:, :w] = v_ref[...]
    va_ref[:, w:] = jnp.ones((va_ref.shape[0], va_ref.shape[1] - w), BF16)


def _exp_scores(s):
    return jnp.exp2(s - jnp.max(s, axis=-1, keepdims=True)).astype(BF16)


def _mla_attn_kernel(q_ref, k_ref, v_ref, *refs, tq, n_cast):
    o_ref = refs[n_cast]
    s0, s1, p0, p1, va_ref = refs[2 * n_cast + 1:]
    _run_casts(refs[:n_cast], refs[n_cast + 1:2 * n_cast + 1])
    sb, pb = (s0, s1), (p0, p1)
    dv = v_ref.shape[1]
    _fill_v_ones(va_ref, v_ref)

    def qk(c, slot):
        sb[slot][...] = lax.dot_general(q_ref[_rows(c, tq), :], k_ref[...], _NT, preferred_element_type=F32)

    def sm(slot):
        pb[slot][...] = _exp_scores(sb[slot][...])

    def pv(c, slot):
        o = jnp.dot(pb[slot][...], va_ref[...], preferred_element_type=F32)
        o_ref[_rows(c, tq), :] = (o[:, :dv] * (1.0 / o[:, dv:])).astype(BF16)

    _pipelined_chunks(q_ref.shape[0] // tq, qk, sm, pv)


def _mla_attn(q, k, v, cast_weights, layer, *, batch, tq):
    t = q.shape[0]
    s = t // batch
    c_in, c_out, c_shapes = _cast_specs(cast_weights, layer, batch * MLA_HEADS, lambda b, h: b * MLA_HEADS + h)
    return pl.pallas_call(
        functools.partial(_mla_attn_kernel, tq=tq, n_cast=len(cast_weights)),
        grid=(batch, MLA_HEADS),
        in_specs=[pl.BlockSpec((s, MLA_QK_PAD), lambda b, h: (b, h)),
                  pl.BlockSpec((s, MLA_QK_PAD), lambda b, h: (b, h)),
                  pl.BlockSpec((s, MLA_V_DIM), lambda b, h: (b, h))] + c_in,
        out_specs=[pl.BlockSpec((s, MLA_V_DIM), lambda b, h: (b, h))] + c_out,
        out_shape=[jax.ShapeDtypeStruct((t, MLA_HEADS * MLA_V_DIM), BF16)] + c_shapes,
        scratch_shapes=[pltpu.VMEM((tq, s), F32), pltpu.VMEM((tq, s), F32),
                        pltpu.VMEM((tq, s), BF16), pltpu.VMEM((tq, s), BF16),
                        pltpu.VMEM((s, 2 * MLA_V_DIM), BF16)],
        compiler_params=_params("parallel", "parallel"),
        name="mla_attn",
    )(q, k, v, *cast_weights)


def _diff_attn_kernel(lq1_ref, lk1_ref, lq2_ref, lk2_ref, gsub_ref, cos_ref, sin_ref, q_ref, k_ref, v_ref,
                      *refs, tq, lam_init, q_scale, n_cast):
    o_ref = refs[n_cast]
    s0, s1, p0, p1, va_ref, kr_ref = refs[2 * n_cast + 1:]
    _run_casts(refs[:n_cast], refs[n_cast + 1:2 * n_cast + 1])
    lam = (jnp.exp(jnp.sum(lq1_ref[...] * lk1_ref[...], axis=-1, keepdims=True))
           - jnp.exp(jnp.sum(lq2_ref[...] * lk2_ref[...], axis=-1, keepdims=True)) + lam_init)
    gain = gsub_ref[...] * (1.0 - lam_init)
    lane = lax.broadcasted_iota(jnp.int32, (tq, 2 * DIFF_HEAD_DIM), 1)
    map0 = lane < DIFF_HEAD_DIM
    sb, pb = (s0, s1), (p0, p1)
    dv = v_ref.shape[1]
    _fill_v_ones(va_ref, v_ref)
    kr_ref[...] = _rope128(k_ref[...].astype(F32), cos_ref[...], sin_ref[...]).astype(BF16)

    def qk(c, slot):
        rows = _rows(c, tq)
        q = _rope128(q_ref[rows, :].astype(F32), cos_ref[rows, :], sin_ref[rows, :]) * q_scale
        q = q.astype(BF16)
        zero = jnp.zeros_like(q)
        q01 = jnp.concatenate([jnp.where(map0, q, zero), jnp.where(map0, zero, q)], axis=0)
        sb[slot][...] = lax.dot_general(q01, kr_ref[...], _NT, preferred_element_type=F32)

    def sm(slot):
        pb[slot][...] = _exp_scores(sb[slot][...])

    def pv(c, slot):
        o = jnp.dot(pb[slot][...], va_ref[...], preferred_element_type=F32)
        o = o[:, :dv] * (1.0 / o[:, dv:])
        o_ref[_rows(c, tq), :] = _rms(o[:tq] - lam * o[tq:], gain).astype(BF16)

    _pipelined_chunks(q_ref.shape[0] // tq, qk, sm, pv)


def _diff_attn(lq1, lk1, lq2, lk2, gsub, cos_t, sin_t, z, cast_weights, layer,
               *, batch, tq, q_col, k_col, v_col, lam_init):
    t = z.shape[0]
    s = t // batch
    hw = 2 * DIFF_HEAD_DIM
    small = lambda a: pl.BlockSpec(a.shape, lambda b, h: (0, 0))
    head = lambda col: pl.BlockSpec((s, hw), lambda b, h: (b, col + h))
    c_in, c_out, c_shapes = _cast_specs(cast_weights, layer, batch * DIFF_HEADS, lambda b, h: b * DIFF_HEADS + h)
    return pl.pallas_call(
        functools.partial(_diff_attn_kernel, tq=tq, lam_init=lam_init, n_cast=len(cast_weights),
                          q_scale=LOG2_E * DIFF_HEAD_DIM ** -0.5),
        grid=(batch, DIFF_HEADS),
        in_specs=[small(lq1), small(lk1), small(lq2), small(lk2), small(gsub),
                  pl.BlockSpec((s, LANES), lambda b, h: (b, 0)), pl.BlockSpec((s, LANES), lambda b, h: (b, 0)),
                  head(q_col), head(k_col), head(v_col)] + c_in,
        out_specs=[pl.BlockSpec((s, DIFF_V_DIM), lambda b, h: (b, h))] + c_out,
        out_shape=[jax.ShapeDtypeStruct((t, DIFF_HEADS * DIFF_V_DIM), BF16)] + c_shapes,
        scratch_shapes=[pltpu.VMEM((2 * tq, s), F32), pltpu.VMEM((2 * tq, s), F32),
                        pltpu.VMEM((2 * tq, s), BF16), pltpu.VMEM((2 * tq, s), BF16),
                        pltpu.VMEM((s, 2 * DIFF_V_DIM), BF16), pltpu.VMEM((s, hw), BF16)],
        compiler_params=_params("parallel", "parallel"),
        name="diff_attn",
    )(lq1, lk1, lq2, lk2, gsub, cos_t, sin_t, z, z, z, *cast_weights)


def _mem_kv_kernel(m_ref, g_ref, w_ref, o_ref, wbf_ref):
    @pl.when(pl.program_id(0) == 0)
    def _():
        wbf_ref[...] = w_ref[...].astype(BF16)

    mn = _rms(m_ref[...], g_ref[...]).astype(BF16)
    o_ref[...] = jnp.dot(mn, wbf_ref[...], preferred_element_type=F32).astype(BF16)


def _mem_kv(mem, g, w, layer, *, tm):
    t, d = mem.shape
    n = w.shape[2]
    return pl.pallas_call(
        _mem_kv_kernel,
        grid=(t // tm,),
        in_specs=[pl.BlockSpec((tm, d), lambda i: (i, 0)),
                  pl.BlockSpec((1, d), lambda i: (0, 0)),
                  pl.BlockSpec((None, d, n), lambda i: (layer, 0, 0), pipeline_mode=pl.Buffered(1))],
        out_specs=pl.BlockSpec((tm, n), lambda i: (i, 0)),
        out_shape=jax.ShapeDtypeStruct((t, n), BF16),
        scratch_shapes=[pltpu.VMEM((d, n), BF16)],
        compiler_params=_params("arbitrary"),
        name="mem_kv",
    )(mem, g, w)


def _merge_cross_kernel(x_ref, oa_ref, ob_ref, ga0_ref, ga1_ref, gb0_ref, gb1_ref, woa_ref, wob_ref, wout_ref,
                        gc_ref, wcq_ref, mkv_ref, wco_ref, o_ref, *, q_scale):
    ya = jnp.dot(oa_ref[...], woa_ref[...], preferred_element_type=F32)
    yb = jnp.dot(ob_ref[...], wob_ref[...], preferred_element_type=F32)
    ga = _sigmoid(jnp.concatenate([ga0_ref[...], ga1_ref[...]], axis=1).astype(F32))
    gb = _sigmoid(jnp.concatenate([gb0_ref[...], gb1_ref[...]], axis=1).astype(F32))
    merged = ga * ya + gb * yb
    x1 = x_ref[...] + jnp.dot(merged.astype(BF16), wout_ref[...], preferred_element_type=F32)

    hn = _rms(x1, gc_ref[...]).astype(BF16)
    qc = (jnp.dot(hn, wcq_ref[...], preferred_element_type=F32) * q_scale).astype(BF16)
    hd = XATTN_HEAD_DIM
    heads = []
    for h in range(XATTN_HEADS):
        k_h = mkv_ref[:, 2 * h * hd:(2 * h + 1) * hd]
        v_h = mkv_ref[:, (2 * h + 1) * hd:(2 * h + 2) * hd]
        s = lax.dot_general(qc[:, h * hd:(h + 1) * hd], k_h, _NT, preferred_element_type=F32)
        e, l = _softmax_parts(s)
        o_h = jnp.dot(e.astype(BF16), v_h, preferred_element_type=F32) * (1.0 / l)
        heads.append(o_h.astype(BF16))
    oc = jnp.concatenate(heads, axis=1)
    o_ref[...] = x1 + jnp.dot(oc, wco_ref[...], preferred_element_type=F32)


def _merge_cross(x, oa, ob, z, gate_block, woa, wob, wout, gc, wcq, mkv, wco, *, tm, batch):
    t, d = x.shape
    blocks_per_seq = t // batch // tm
    m = mkv.shape[0] // batch
    const = lambda a: pl.BlockSpec(a.shape, lambda i: (0, 0), pipeline_mode=pl.Buffered(1))
    rows = lambda width: pl.BlockSpec((tm, width), lambda i: (i, 0))
    return pl.pallas_call(
        functools.partial(_merge_cross_kernel, q_scale=LOG2_E * XATTN_HEAD_DIM ** -0.5),
        grid=(t // tm,),
        in_specs=[rows(d), rows(oa.shape[1]), rows(ob.shape[1])]
                 + [pl.BlockSpec((tm, d // 2), lambda i, k=k: (i, gate_block + k)) for k in range(4)]
                 + [const(woa), const(wob), const(wout), const(gc), const(wcq),
                  pl.BlockSpec((m, mkv.shape[1]), lambda i: (i // blocks_per_seq, 0)),
                  const(wco)],
        out_specs=rows(d),
        out_shape=jax.ShapeDtypeStruct((t, d), F32),
        compiler_params=_params("parallel"),
        name="merge_cross",
    )(x, oa, ob, z, z, z, z, woa, wob, wout, gc, wcq, mkv, wco)


def _conv_ffn_kernel(xp_ref, x_ref, xn_ref, g_ref, wg_ref, wv_ref, cg_ref, cv_ref,
                     wd_ref, gfin_ref, o_ref, h_ref, ug_ref, uv_ref, acc_ref,
                     *, tm, halo, blocks_per_seq, final_norm):
    i, f = pl.program_id(0), pl.program_id(1)

    @pl.when(f == 0)
    def _():
        g = g_ref[...]
        pos = i % blocks_per_seq
        keep_prev = jnp.where(pos == 0, 0.0, 1.0)
        keep_next = jnp.where(pos == blocks_per_seq - 1, 0.0, 1.0)
        h_ref[0:halo, :] = (_rms(xp_ref[...], g) * keep_prev).astype(BF16)
        h_ref[halo:halo + tm, :] = _rms(x_ref[...], g).astype(BF16)
        h_ref[halo + tm:, :] = (_rms(xn_ref[...], g) * keep_next).astype(BF16)
        acc_ref[...] = jnp.zeros_like(acc_ref)

    h = h_ref[...]
    ug_ref[...] = jnp.dot(h, wg_ref[...], preferred_element_type=F32)
    uv_ref[...] = jnp.dot(h, wv_ref[...], preferred_element_type=F32)

    def conv(u_ref, c_ref):
        c = c_ref[...]
        return (u_ref[halo - 1:halo - 1 + tm, :] * c[0:1, :] + u_ref[halo:halo + tm, :] * c[1:2, :]
                + u_ref[halo + 1:halo + 1 + tm, :] * c[2:3, :] + c[3:4, :])

    yg = conv(ug_ref, cg_ref)
    yv = conv(uv_ref, cv_ref)
    act = (yg * _sigmoid(yg) * yv).astype(BF16)
    acc_ref[...] += jnp.dot(act, wd_ref[...], preferred_element_type=F32)

    @pl.when(f == pl.num_programs(1) - 1)
    def _():
        y = x_ref[...] + acc_ref[...]
        o_ref[...] = _rms(y, gfin_ref[...]) if final_norm else y


def _conv_ffn(x, g, w_up, conv_w, conv_b, w_down, g_final, *, tm, tf, batch, final_norm):
    t, d = x.shape
    ffn = w_down.shape[0]
    nf = ffn // tf
    halo = BF16_SUBLANES
    blocks_per_seq = t // batch // tm
    hb = tm // halo
    last_halo_block = t // halo - 1
    kern = functools.partial(_conv_ffn_kernel, tm=tm, halo=halo, blocks_per_seq=blocks_per_seq,
                             final_norm=final_norm)
    conv_wb = jnp.concatenate([conv_w, conv_b], axis=0)
    return pl.pallas_call(
        kern,
        grid=(t // tm, nf),
        in_specs=[pl.BlockSpec((halo, d), lambda i, f: (jnp.maximum(i * hb - 1, 0), 0)),
                  pl.BlockSpec((tm, d), lambda i, f: (i, 0)),
                  pl.BlockSpec((halo, d), lambda i, f: (jnp.minimum((i + 1) * hb, last_halo_block), 0)),
                  pl.BlockSpec((1, d), lambda i, f: (0, 0)),
                  pl.BlockSpec((d, tf), lambda i, f: (0, f)),
                  pl.BlockSpec((d, tf), lambda i, f: (0, nf + f)),
                  pl.BlockSpec((conv_wb.shape[0], tf), lambda i, f: (0, f)),
                  pl.BlockSpec((conv_wb.shape[0], tf), lambda i, f: (0, nf + f)),
                  pl.BlockSpec((tf, d), lambda i, f: (f, 0)),
                  pl.BlockSpec((1, d), lambda i, f: (0, 0))],
        out_specs=pl.BlockSpec((tm, d), lambda i, f: (i, 0)),
        out_shape=jax.ShapeDtypeStruct((t, d), F32),
        scratch_shapes=[pltpu.VMEM((tm + 2 * halo, d), BF16),
                        pltpu.VMEM((tm + 2 * halo, tf), F32),
                        pltpu.VMEM((tm + 2 * halo, tf), F32),
                        pltpu.VMEM((tm, d), F32)],
        compiler_params=_params("parallel", "arbitrary"),
        name="conv_ffn",
    )(x, x, x, g, w_up, w_up, conv_wb, conv_wb, w_down, g_final)


def _rope_tables(positions):
    dim = MLA_ROPE_DIM
    inv = ROPE_THETA ** (-jnp.arange(0, dim, 2, dtype=F32) / dim)
    ang = positions.astype(F32)[..., None] * inv
    cos, sin = jnp.cos(ang), jnp.sin(ang)
    reps = LANES // dim
    cos_t = jnp.concatenate([cos, cos] * reps, axis=-1)
    sin_t = jnp.concatenate([-sin, sin] * reps, axis=-1)
    return cos_t.reshape(-1, LANES), sin_t.reshape(-1, LANES)


def kernel(x, mem, positions, g_mix_norm, w_in, g_q_norm, w_uq, g_kv_norm, w_ukv, w_o_mla, lambda_q1, lambda_k1, lambda_q2, lambda_k2, g_diff_sub, w_o_diff, w_out, g_cross_norm, g_mem_norm, w_cross_q, w_cross_kv, w_cross_o, g_ffn_norm, w_up, conv_w, conv_b, w_down, g_final):
    batch, seq, d = x.shape
    depth = w_in.shape[0]
    t = batch * seq
    assert depth >= 1 and MLA_ROPE_DIM == DIFF_HEAD_DIM and MLA_NOPE_DIM == MLA_V_DIM
    q_rank, kv_rank = g_q_norm.shape[1], g_kv_norm.shape[1]
    dqk = DIFF_HEADS * 2 * DIFF_HEAD_DIM
    dv_w = DIFF_HEADS * DIFF_V_DIM
    c0 = q_rank + kv_rank + MLA_ROPE_DIM
    tn = 1024
    assert dqk == tn and dv_w == tn and (2 * d) % tn == 0

    cos_t, sin_t = _rope_tables(positions)
    xf = x.reshape(t, d)
    memf = mem.reshape(-1, d)
    row = lambda v: v.reshape(1, -1)
    w_in_t = jnp.swapaxes(w_in, 1, 2)

    for l in range(depth):
        lam_init = 0.8 - 0.6 * math.exp(-0.3 * l)
        wq = jnp.pad(w_uq[l].reshape(q_rank, MLA_HEADS, MLA_NOPE_DIM + MLA_ROPE_DIM),
                     ((0, 0), (0, 0), (0, MLA_QK_PAD - MLA_NOPE_DIM - MLA_ROPE_DIM)))
        wq = wq.reshape(q_rank, MLA_HEADS * MLA_QK_PAD).astype(BF16)
        wkv = w_ukv[l].astype(BF16)

        q, k, v, xn = _mla_proj(xf, row(g_mix_norm[l]), w_in_t, l, c0 + LANES - MLA_ROPE_DIM,
                                row(g_q_norm[l]), row(g_kv_norm[l]), wq, wkv, cos_t, sin_t, tm=512)
        z = _in_proj(xn, w_in_t, l, tm=2048, tn=tn, first_row=c0, n_blocks=3 + 2 * d // tn)
        oa, w_up_b, w_down_b = _mla_attn(q, k, v, [w_up, w_down], l, batch=batch, tq=512)
        ob, w_oa_b, w_ob_b, w_out_b, w_cq_b, w_co_b = _diff_attn(
            row(lambda_q1[l]), row(lambda_k1[l]), row(lambda_q2[l]), row(lambda_k2[l]), row(g_diff_sub[l]),
            cos_t, sin_t, z, [w_o_mla, w_o_diff, w_out, w_cross_q, w_cross_o], l, batch=batch, tq=512,
            q_col=0, k_col=DIFF_HEADS, v_col=2 * DIFF_HEADS, lam_init=lam_init)
        mkv = _mem_kv(memf, row(g_mem_norm[l]), w_cross_kv, l, tm=memf.shape[0] // batch)
        xf = _merge_cross(xf, oa, ob, z, 3 * tn // (d // 2), w_oa_b, w_ob_b, w_out_b, row(g_cross_norm[l]), w_cq_b, mkv,
                          w_co_b, tm=512, batch=batch)
        xf = _conv_ffn(xf, row(g_ffn_norm[l]), w_up_b, conv_w[l], row(conv_b[l]),
                       w_down_b, row(g_final), tm=512, tf=512, batch=batch,
                       final_norm=(l == depth - 1))
    return xf.reshape(batch, seq, d)
```

```python
import functools
import math

import jax
import jax.numpy as jnp
from jax import lax
from jax.experimental import pallas as pl
from jax.experimental.pallas import tpu as pltpu

F32 = jnp.float32
BF16 = jnp.bfloat16

EPS = 1e-6
LOG2_E = math.log2(math.e)
ROPE_THETA = 10000.0
MLA_HEADS = 8
MLA_NOPE_DIM = 128
MLA_ROPE_DIM = 64
MLA_V_DIM = 128
MLA_QK_PAD = 256
DIFF_HEADS = 8
DIFF_HEAD_DIM = 64
DIFF_V_DIM = 2 * DIFF_HEAD_DIM
XATTN_HEADS = 4
XATTN_HEAD_DIM = 128
LANES = 128
BF16_SUBLANES = 16
MXU_COLS = 256
VMEM_LIMIT_BYTES = 56 * 1024 * 1024

_NT = (((1,), (1,)), ((), ()))


def _params(*semantics):
    return pltpu.CompilerParams(dimension_semantics=semantics, vmem_limit_bytes=VMEM_LIMIT_BYTES)


def _rms(x, g):
    ms = jnp.mean(x * x, axis=-1, keepdims=True)
    return x * lax.rsqrt(ms + EPS) * g


def _sigmoid(x):
    return 1.0 / (1.0 + jnp.exp(-x))


def _rope128(x, cos_t, sin_t):
    lane = lax.broadcasted_iota(jnp.int32, x.shape, 1)
    first_half = (lane & (MLA_ROPE_DIM // 2)) == 0
    partner = jnp.where(first_half, pltpu.roll(x, LANES - 32, 1), pltpu.roll(x, 32, 1))
    return x * cos_t + partner * sin_t


def _softmax_parts(s):
    m = jnp.max(s, axis=-1, keepdims=True)
    e = jnp.exp2(s - m)
    return e, jnp.sum(e, axis=-1, keepdims=True)


def _mla_proj_kernel(x_ref, g_ref, w1t_ref, gq_ref, gkv_ref, wq_ref, wkv_ref, cos_ref, sin_ref,
                     q_ref, k_ref, v_ref, xn_ref, w1_ref, *, q_rank, kv_rank, q_scale):
    @pl.when(pl.program_id(0) == 0)
    def _():
        w1_ref[...] = w1t_ref[...].astype(BF16)

    xn = _rms(x_ref[...], g_ref[...]).astype(BF16)
    xn_ref[...] = xn
    z = lax.dot_general(xn, w1_ref[...], _NT, preferred_element_type=F32)
    cqn = _rms(z[:, :q_rank], gq_ref[...]).astype(BF16)
    ckvn = _rms(z[:, q_rank:q_rank + kv_rank], gkv_ref[...]).astype(BF16)
    cos_t, sin_t = cos_ref[...], sin_ref[...]
    kpe = _rope128(z[:, q_rank + kv_rank:], cos_t, sin_t).astype(BF16)
    q = jnp.dot(cqn, wq_ref[...], preferred_element_type=F32) * q_scale
    kv = jnp.dot(ckvn, wkv_ref[...], preferred_element_type=F32)
    kv_w = MLA_NOPE_DIM + MLA_V_DIM
    for h in range(MLA_HEADS):
        lo, mid, hi = h * MLA_QK_PAD, h * MLA_QK_PAD + MLA_NOPE_DIM, (h + 1) * MLA_QK_PAD
        q_ref[:, lo:mid] = q[:, lo:mid].astype(BF16)
        q_ref[:, mid:hi] = _rope128(q[:, mid:hi], cos_t, sin_t).astype(BF16)
        k_ref[:, lo:mid] = kv[:, kv_w * h:kv_w * h + MLA_NOPE_DIM].astype(BF16)
        k_ref[:, mid:hi] = kpe
        v_ref[:, h * MLA_V_DIM:(h + 1) * MLA_V_DIM] = kv[:, kv_w * h + MLA_NOPE_DIM:kv_w * (h + 1)].astype(BF16)


def _mla_proj(x, g, w_t, layer, n1, gq, gkv, wq, wkv, cos_t, sin_t, *, tm):
    t, d = x.shape
    q_rank, kv_rank = gq.shape[1], gkv.shape[1]
    full = lambda a: pl.BlockSpec(a.shape, lambda i: (0, 0))
    rows = lambda width: pl.BlockSpec((tm, width), lambda i: (i, 0))
    qk_w, v_w = MLA_HEADS * MLA_QK_PAD, MLA_HEADS * MLA_V_DIM
    kern = functools.partial(_mla_proj_kernel, q_rank=q_rank, kv_rank=kv_rank,
                             q_scale=LOG2_E * (MLA_NOPE_DIM + MLA_ROPE_DIM) ** -0.5)
    return pl.pallas_call(
        kern,
        grid=(t // tm,),
        in_specs=[rows(d), full(g),
                  pl.BlockSpec((None, n1, d), lambda i: (layer, 0, 0), pipeline_mode=pl.Buffered(1)),
                  full(gq), full(gkv), full(wq), full(wkv), rows(LANES), rows(LANES)],
        out_specs=[rows(qk_w), rows(qk_w), rows(v_w), rows(d)],
        out_shape=[jax.ShapeDtypeStruct((t, qk_w), BF16), jax.ShapeDtypeStruct((t, qk_w), BF16),
                   jax.ShapeDtypeStruct((t, v_w), BF16), jax.ShapeDtypeStruct((t, d), BF16)],
        scratch_shapes=[pltpu.VMEM((n1, d), BF16)],
        compiler_params=_params("arbitrary"),
        name="mla_proj",
    )(x, g, w_t, gq, gkv, wq, wkv, cos_t, sin_t)


def _in_proj_kernel(xn_ref, wt_ref, o_ref, wbf_ref):
    @pl.when(pl.program_id(1) == 0)
    def _():
        wbf_ref[...] = wt_ref[0].astype(BF16)

    for c in range(0, o_ref.shape[1], MXU_COLS):
        z = lax.dot_general(xn_ref[...], wbf_ref[c:c + MXU_COLS, :], _NT, preferred_element_type=F32)
        o_ref[:, c:c + MXU_COLS] = z.astype(BF16)


def _in_proj(xn, w_t, layer, *, tm, tn, first_row, n_blocks):
    t, d = xn.shape
    w_spec = pl.BlockSpec((pl.Element(1), pl.Element(tn), pl.Element(d)),
                          lambda j, i: (layer, pl.multiple_of(first_row + j * tn, BF16_SUBLANES), 0))
    return pl.pallas_call(
        _in_proj_kernel,
        grid=(n_blocks, t // tm),
        in_specs=[pl.BlockSpec((tm, d), lambda j, i: (i, 0)), w_spec],
        out_specs=pl.BlockSpec((tm, tn), lambda j, i: (i, j)),
        out_shape=jax.ShapeDtypeStruct((t, n_blocks * tn), BF16),
        scratch_shapes=[pltpu.VMEM((tn, d), BF16)],
        compiler_params=_params("arbitrary", "arbitrary"),
        name="in_proj",
    )(xn, w_t)


def _cast_specs(weights, layer, steps, step_index):
    ins, outs, shapes = [], [], []
    for w in weights:
        _, r, c = w.shape
        rb = r // steps
        assert rb * steps == r and rb % BF16_SUBLANES == 0
        ins.append(pl.BlockSpec((None, rb, c), lambda *g: (layer, step_index(*g), 0)))
        outs.append(pl.BlockSpec((rb, c), lambda *g: (step_index(*g), 0)))
        shapes.append(jax.ShapeDtypeStruct((r, c), BF16))
    return ins, outs, shapes


def _run_casts(src_refs, dst_refs):
    for src, dst in zip(src_refs, dst_refs):
        dst[...] = src[...].astype(BF16)


def _rows(c, tq):
    if isinstance(c, int):
        return slice(c * tq, (c + 1) * tq)
    return pl.ds(pl.multiple_of(c * tq, tq), tq)


def _pipelined_chunks(n, qk, sm, pv):
    def step(c, parity):
        qk(c, parity)
        sm(1 - parity)
        pv(c - 2, parity)

    qk(0, 0)
    qk(1, 1)
    sm(0)

    def body(i, carry):
        step(2 * i, 0)
        step(2 * i + 1, 1)
        return carry

    lax.fori_loop(1, n // 2, body, 0)
    sm(1)
    pv(n - 2, 0)
    pv(n - 1, 1)


def _fill_v_ones(va_ref, v_ref):
    w = v_ref.shape[1]
    va_ref[:, :w] = v_ref[...]
    va_ref[:, w:] = jnp.ones((va_ref.shape[0], va_ref.shape[1] - w), BF16)


def _exp_scores(s):
    return jnp.exp2(s - jnp.max(s, axis=-1, keepdims=True)).astype(BF16)


def _mla_attn_kernel(q_ref, k_ref, v_ref, *refs, tq, n_cast):
    o_ref = refs[n_cast]
    s0, s1, p0, p1, va_ref = refs[2 * n_cast + 1:]
    _run_casts(refs[:n_cast], refs[n_cast + 1:2 * n_cast + 1])
    sb, pb = (s0, s1), (p0, p1)
    dv = v_ref.shape[1]
    _fill_v_ones(va_ref, v_ref)

    def qk(c, slot):
        sb[slot][...] = lax.dot_general(q_ref[_rows(c, tq), :], k_ref[...], _NT, preferred_element_type=F32)

    def sm(slot):
        pb[slot][...] = _exp_scores(sb[slot][...])

    def pv(c, slot):
        o = jnp.dot(pb[slot][...], va_ref[...], preferred_element_type=F32)
        o_ref[_rows(c, tq), :] = (o[:, :dv] * (1.0 / o[:, dv:])).astype(BF16)

    _pipelined_chunks(q_ref.shape[0] // tq, qk, sm, pv)


def _mla_attn(q, k, v, cast_weights, layer, *, batch, tq):
    t = q.shape[0]
    s = t // batch
    c_in, c_out, c_shapes = _cast_specs(cast_weights, layer, batch * MLA_HEADS, lambda b, h: b * MLA_HEADS + h)
    return pl.pallas_call(
        functools.partial(_mla_attn_kernel, tq=tq, n_cast=len(cast_weights)),
        grid=(batch, MLA_HEADS),
        in_specs=[pl.BlockSpec((s, MLA_QK_PAD), lambda b, h: (b, h)),
                  pl.BlockSpec((s, MLA_QK_PAD), lambda b, h: (b, h)),
                  pl.BlockSpec((s, MLA_V_DIM), lambda b, h: (b, h))] + c_in,
        out_specs=[pl.BlockSpec((s, MLA_V_DIM), lambda b, h: (b, h))] + c_out,
        out_shape=[jax.ShapeDtypeStruct((t, MLA_HEADS * MLA_V_DIM), BF16)] + c_shapes,
        scratch_shapes=[pltpu.VMEM((tq, s), F32), pltpu.VMEM((tq, s), F32),
                        pltpu.VMEM((tq, s), BF16), pltpu.VMEM((tq, s), BF16),
                        pltpu.VMEM((s, 2 * MLA_V_DIM), BF16)],
        compiler_params=_params("parallel", "parallel"),
        name="mla_attn",
    )(q, k, v, *cast_weights)


def _diff_attn_kernel(lq1_ref, lk1_ref, lq2_ref, lk2_ref, gsub_ref, cos_ref, sin_ref, q_ref, k_ref, v_ref,
                      *refs, tq, lam_init, q_scale, n_cast):
    o_ref = refs[n_cast]
    s0, s1, p0, p1, va_ref, kr_ref = refs[2 * n_cast + 1:]
    _run_casts(refs[:n_cast], refs[n_cast + 1:2 * n_cast + 1])
    lam = (jnp.exp(jnp.sum(lq1_ref[...] * lk1_ref[...], axis=-1, keepdims=True))
           - jnp.exp(jnp.sum(lq2_ref[...] * lk2_ref[...], axis=-1, keepdims=True)) + lam_init)
    gain = gsub_ref[...] * (1.0 - lam_init)
    lane = lax.broadcasted_iota(jnp.int32, (tq, 2 * DIFF_HEAD_DIM), 1)
    map0 = lane < DIFF_HEAD_DIM
    sb, pb = (s0, s1), (p0, p1)
    dv = v_ref.shape[1]
    _fill_v_ones(va_ref, v_ref)
    kr_ref[...] = _rope128(k_ref[...].astype(F32), cos_ref[...], sin_ref[...]).astype(BF16)

    def qk(c, slot):
        rows = _rows(c, tq)
        q = _rope128(q_ref[rows, :].astype(F32), cos_ref[rows, :], sin_ref[rows, :]) * q_scale
        q = q.astype(BF16)
        zero = jnp.zeros_like(q)
        q01 = jnp.concatenate([jnp.where(map0, q, zero), jnp.where(map0, zero, q)], axis=0)
        sb[slot][...] = lax.dot_general(q01, kr_ref[...], _NT, preferred_element_type=F32)

    def sm(slot):
        pb[slot][...] = _exp_scores(sb[slot][...])

    def pv(c, slot):
        o = jnp.dot(pb[slot][...], va_ref[...], preferred_element_type=F32)
        o = o[:, :dv] * (1.0 / o[:, dv:])
        o_ref[_rows(c, tq), :] = _rms(o[:tq] - lam * o[tq:], gain).astype(BF16)

    _pipelined_chunks(q_ref.shape[0] // tq, qk, sm, pv)


def _diff_attn(lq1, lk1, lq2, lk2, gsub, cos_t, sin_t, z, cast_weights, layer,
               *, batch, tq, q_col, k_col, v_col, lam_init):
    t = z.shape[0]
    s = t // batch
    hw = 2 * DIFF_HEAD_DIM
    small = lambda a: pl.BlockSpec(a.shape, lambda b, h: (0, 0))
    head = lambda col: pl.BlockSpec((s, hw), lambda b, h: (b, col + h))
    c_in, c_out, c_shapes = _cast_specs(cast_weights, layer, batch * DIFF_HEADS, lambda b, h: b * DIFF_HEADS + h)
    return pl.pallas_call(
        functools.partial(_diff_attn_kernel, tq=tq, lam_init=lam_init, n_cast=len(cast_weights),
                          q_scale=LOG2_E * DIFF_HEAD_DIM ** -0.5),
        grid=(batch, DIFF_HEADS),
        in_specs=[small(lq1), small(lk1), small(lq2), small(lk2), small(gsub),
                  pl.BlockSpec((s, LANES), lambda b, h: (b, 0)), pl.BlockSpec((s, LANES), lambda b, h: (b, 0)),
                  head(q_col), head(k_col), head(v_col)] + c_in,
        out_specs=[pl.BlockSpec((s, DIFF_V_DIM), lambda b, h: (b, h))] + c_out,
        out_shape=[jax.ShapeDtypeStruct((t, DIFF_HEADS * DIFF_V_DIM), BF16)] + c_shapes,
        scratch_shapes=[pltpu.VMEM((2 * tq, s), F32), pltpu.VMEM((2 * tq, s), F32),
                        pltpu.VMEM((2 * tq, s), BF16), pltpu.VMEM((2 * tq, s), BF16),
                        pltpu.VMEM((s, 2 * DIFF_V_DIM), BF16), pltpu.VMEM((s, hw), BF16)],
        compiler_params=_params("parallel", "parallel"),
        name="diff_attn",
    )(lq1, lk1, lq2, lk2, gsub, cos_t, sin_t, z, z, z, *cast_weights)


def _mem_kv_kernel(m_ref, g_ref, w_ref, o_ref, wbf_ref):
    @pl.when(pl.program_id(0) == 0)
    def _():
        wbf_ref[...] = w_ref[...].astype(BF16)

    mn = _rms(m_ref[...], g_ref[...]).astype(BF16)
    o_ref[...] = jnp.dot(mn, wbf_ref[...], preferred_element_type=F32).astype(BF16)


def _mem_kv(mem, g, w, layer, *, tm):
    t, d = mem.shape
    n = w.shape[2]
    return pl.pallas_call(
        _mem_kv_kernel,
        grid=(t // tm,),
        in_specs=[pl.BlockSpec((tm, d), lambda i: (i, 0)),
                  pl.BlockSpec((1, d), lambda i: (0, 0)),
                  pl.BlockSpec((None, d, n), lambda i: (layer, 0, 0), pipeline_mode=pl.Buffered(1))],
        out_specs=pl.BlockSpec((tm, n), lambda i: (i, 0)),
        out_shape=jax.ShapeDtypeStruct((t, n), BF16),
        scratch_shapes=[pltpu.VMEM((d, n), BF16)],
        compiler_params=_params("arbitrary"),
        name="mem_kv",
    )(mem, g, w)


def _merge_cross_kernel(x_ref, oa_ref, ob_ref, ga0_ref, ga1_ref, gb0_ref, gb1_ref, woa_ref, wob_ref, wout_ref,
                        gc_ref, wcq_ref, mkv_ref, wco_ref, o_ref, *, q_scale):
    ya = jnp.dot(oa_ref[...], woa_ref[...], preferred_element_type=F32)
    yb = jnp.dot(ob_ref[...], wob_ref[...], preferred_element_type=F32)
    ga = _sigmoid(jnp.concatenate([ga0_ref[...], ga1_ref[...]], axis=1).astype(F32))
    gb = _sigmoid(jnp.concatenate([gb0_ref[...], gb1_ref[...]], axis=1).astype(F32))
    merged = ga * ya + gb * yb
    x1 = x_ref[...] + jnp.dot(merged.astype(BF16), wout_ref[...], preferred_element_type=F32)

    hn = _rms(x1, gc_ref[...]).astype(BF16)
    qc = (jnp.dot(hn, wcq_ref[...], preferred_element_type=F32) * q_scale).astype(BF16)
    hd = XATTN_HEAD_DIM
    heads = []
    for h in range(XATTN_HEADS):
        k_h = mkv_ref[:, 2 * h * hd:(2 * h + 1) * hd]
        v_h = mkv_ref[:, (2 * h + 1) * hd:(2 * h + 2) * hd]
        s = lax.dot_general(qc[:, h * hd:(h + 1) * hd], k_h, _NT, preferred_element_type=F32)
        e, l = _softmax_parts(s)
        o_h = jnp.dot(e.astype(BF16), v_h, preferred_element_type=F32) * (1.0 / l)
        heads.append(o_h.astype(BF16))
    oc = jnp.concatenate(heads, axis=1)
    o_ref[...] = x1 + jnp.dot(oc, wco_ref[...], preferred_element_type=F32)


def _merge_cross(x, oa, ob, z, gate_block, woa, wob, wout, gc, wcq, mkv, wco, *, tm, batch):
    t, d = x.shape
    blocks_per_seq = t // batch // tm
    m = mkv.shape[0] // batch
    const = lambda a: pl.BlockSpec(a.shape, lambda i: (0, 0), pipeline_mode=pl.Buffered(1))
    rows = lambda width: pl.BlockSpec((tm, width), lambda i: (i, 0))
    return pl.pallas_call(
        functools.partial(_merge_cross_kernel, q_scale=LOG2_E * XATTN_HEAD_DIM ** -0.5),
        grid=(t // tm,),
        in_specs=[rows(d), rows(oa.shape[1]), rows(ob.shape[1])]
                 + [pl.BlockSpec((tm, d // 2), lambda i, k=k: (i, gate_block + k)) for k in range(4)]
                 + [const(woa), const(wob), const(wout), const(gc), const(wcq),
                  pl.BlockSpec((m, mkv.shape[1]), lambda i: (i // blocks_per_seq, 0)),
                  const(wco)],
        out_specs=rows(d),
        out_shape=jax.ShapeDtypeStruct((t, d), F32),
        compiler_params=_params("parallel"),
        name="merge_cross",
    )(x, oa, ob, z, z, z, z, woa, wob, wout, gc, wcq, mkv, wco)


def _conv_ffn_kernel(xp_ref, x_ref, xn_ref, g_ref, wga_ref, wgb_ref, wva_ref, wvb_ref, c_ref, wda_ref, wdb_ref,
                     gfin_ref, o_ref, h_ref, uga_ref, uva_ref, ugb_ref, uvb_ref, acc_ref,
                     *, tm, halo, blocks_per_seq, n_chunks, final_norm):
    i, f = pl.program_id(0), pl.program_id(1)
    last = pl.num_programs(1) - 1

    @pl.when(f == 0)
    def _():
        g = g_ref[...]
        pos = i % blocks_per_seq
        keep_prev = jnp.where(pos == 0, 0.0, 1.0)
        keep_next = jnp.where(pos == blocks_per_seq - 1, 0.0, 1.0)
        h_ref[0:halo, :] = (_rms(xp_ref[...], g) * keep_prev).astype(BF16)
        h_ref[halo:halo + tm, :] = _rms(x_ref[...], g).astype(BF16)
        h_ref[halo + tm:, :] = (_rms(xn_ref[...], g) * keep_next).astype(BF16)
        acc_ref[...] = jnp.zeros_like(acc_ref)

    def conv(u_ref, c):
        return (u_ref[halo - 1:halo - 1 + tm, :] * c[0:1, :] + u_ref[halo:halo + tm, :] * c[1:2, :]
                + u_ref[halo + 1:halo + 1 + tm, :] * c[2:3, :] + c[3:4, :])

    def chunk(idx, wg_ref, wv_ref, wd_ref, ug_ref, uv_ref):
        h = h_ref[...]
        ug_ref[...] = jnp.dot(h, wg_ref[...], preferred_element_type=F32)
        uv_ref[...] = jnp.dot(h, wv_ref[...], preferred_element_type=F32)
        yg = conv(ug_ref, c_ref[idx])
        yv = conv(uv_ref, c_ref[n_chunks + idx])
        act = (yg * _sigmoid(yg) * yv).astype(BF16)
        return jnp.dot(act, wd_ref[...], preferred_element_type=F32)

    @pl.when(f < last)
    def _():
        acc_ref[...] += (chunk(2 * f, wga_ref, wva_ref, wda_ref, uga_ref, uva_ref)
                         + chunk(2 * f + 1, wgb_ref, wvb_ref, wdb_ref, ugb_ref, uvb_ref))

    @pl.when(f == last)
    def _():
        y = x_ref[...] + acc_ref[...] + chunk(2 * f, wga_ref, wva_ref, wda_ref, uga_ref, uva_ref)
        o_ref[...] = _rms(y, gfin_ref[...]) if final_norm else y


def _conv_ffn(x, g, w_up, conv_w, conv_b, w_down, g_final, *, tm, tf, batch, final_norm):
    t, d = x.shape
    ffn = w_down.shape[0]
    n_chunks = ffn // tf
    assert n_chunks % 2 == 1
    steps = n_chunks // 2 + 1
    halo = BF16_SUBLANES
    blocks_per_seq = t // batch // tm
    hb = tm // halo
    last_halo_block = t // halo - 1
    kern = functools.partial(_conv_ffn_kernel, tm=tm, halo=halo, blocks_per_seq=blocks_per_seq,
                             n_chunks=n_chunks, final_norm=final_norm)
    conv_wb = jnp.concatenate([conv_w, conv_b], axis=0)
    conv_wb = conv_wb.reshape(conv_wb.shape[0], 2 * n_chunks, tf).transpose(1, 0, 2)
    ca = lambda f: 2 * f
    cb = lambda f: jnp.minimum(2 * f + 1, n_chunks - 1)
    return pl.pallas_call(
        kern,
        grid=(t // tm, steps),
        in_specs=[pl.BlockSpec((halo, d), lambda i, f: (jnp.maximum(i * hb - 1, 0), 0)),
                  pl.BlockSpec((tm, d), lambda i, f: (i, 0)),
                  pl.BlockSpec((halo, d), lambda i, f: (jnp.minimum((i + 1) * hb, last_halo_block), 0)),
                  pl.BlockSpec((1, d), lambda i, f: (0, 0)),
                  pl.BlockSpec((d, tf), lambda i, f: (0, ca(f))),
                  pl.BlockSpec((d, tf), lambda i, f: (0, cb(f))),
                  pl.BlockSpec((d, tf), lambda i, f: (0, n_chunks + ca(f))),
                  pl.BlockSpec((d, tf), lambda i, f: (0, n_chunks + cb(f))),
                  pl.BlockSpec(conv_wb.shape, lambda i, f: (0, 0, 0)),
                  pl.BlockSpec((tf, d), lambda i, f: (ca(f), 0)),
                  pl.BlockSpec((tf, d), lambda i, f: (cb(f), 0)),
                  pl.BlockSpec((1, d), lambda i, f: (0, 0))],
        out_specs=pl.BlockSpec((tm, d), lambda i, f: (i, 0)),
        out_shape=jax.ShapeDtypeStruct((t, d), F32),
        scratch_shapes=[pltpu.VMEM((tm + 2 * halo, d), BF16)]
                       + [pltpu.VMEM((tm + 2 * halo, tf), F32)] * 4
                       + [pltpu.VMEM((tm, d), F32)],
        compiler_params=_params("parallel", "arbitrary"),
        name="conv_ffn",
    )(x, x, x, g, w_up, w_up, w_up, w_up, conv_wb, w_down, w_down, g_final)


def _rope_tables(positions):
    dim = MLA_ROPE_DIM
    inv = ROPE_THETA ** (-jnp.arange(0, dim, 2, dtype=F32) / dim)
    ang = positions.astype(F32)[..., None] * inv
    cos, sin = jnp.cos(ang), jnp.sin(ang)
    reps = LANES // dim
    cos_t = jnp.concatenate([cos, cos] * reps, axis=-1)
    sin_t = jnp.concatenate([-sin, sin] * reps, axis=-1)
    return cos_t.reshape(-1, LANES), sin_t.reshape(-1, LANES)


def kernel(x, mem, positions, g_mix_norm, w_in, g_q_norm, w_uq, g_kv_norm, w_ukv, w_o_mla, lambda_q1, lambda_k1, lambda_q2, lambda_k2, g_diff_sub, w_o_diff, w_out, g_cross_norm, g_mem_norm, w_cross_q, w_cross_kv, w_cross_o, g_ffn_norm, w_up, conv_w, conv_b, w_down, g_final):
    batch, seq, d = x.shape
    depth = w_in.shape[0]
    t = batch * seq
    assert depth >= 1 and MLA_ROPE_DIM == DIFF_HEAD_DIM and MLA_NOPE_DIM == MLA_V_DIM
    q_rank, kv_rank = g_q_norm.shape[1], g_kv_norm.shape[1]
    dqk = DIFF_HEADS * 2 * DIFF_HEAD_DIM
    dv_w = DIFF_HEADS * DIFF_V_DIM
    c0 = q_rank + kv_rank + MLA_ROPE_DIM
    tn = 1024
    assert dqk == tn and dv_w == tn and (2 * d) % tn == 0

    cos_t, sin_t = _rope_tables(positions)
    xf = x.reshape(t, d)
    memf = mem.reshape(-1, d)
    row = lambda v: v.reshape(1, -1)
    w_in_t = jnp.swapaxes(w_in, 1, 2)

    for l in range(depth):
        lam_init = 0.8 - 0.6 * math.exp(-0.3 * l)
        wq = jnp.pad(w_uq[l].reshape(q_rank, MLA_HEADS, MLA_NOPE_DIM + MLA_ROPE_DIM),
                     ((0, 0), (0, 0), (0, MLA_QK_PAD - MLA_NOPE_DIM - MLA_ROPE_DIM)))
        wq = wq.reshape(q_rank, MLA_HEADS * MLA_QK_PAD).astype(BF16)
        wkv = w_ukv[l].astype(BF16)

        q, k, v, xn = _mla_proj(xf, row(g_mix_norm[l]), w_in_t, l, c0 + LANES - MLA_ROPE_DIM,
                                row(g_q_norm[l]), row(g_kv_norm[l]), wq, wkv, cos_t, sin_t, tm=512)
        z = _in_proj(xn, w_in_t, l, tm=2048, tn=tn, first_row=c0, n_blocks=3 + 2 * d // tn)
        oa, w_up_b, w_down_b = _mla_attn(q, k, v, [w_up, w_down], l, batch=batch, tq=512)
        ob, w_oa_b, w_ob_b, w_out_b, w_cq_b, w_co_b = _diff_attn(
            row(lambda_q1[l]), row(lambda_k1[l]), row(lambda_q2[l]), row(lambda_k2[l]), row(g_diff_sub[l]),
            cos_t, sin_t, z, [w_o_mla, w_o_diff, w_out, w_cross_q, w_cross_o], l, batch=batch, tq=512,
            q_col=0, k_col=DIFF_HEADS, v_col=2 * DIFF_HEADS, lam_init=lam_init)
        mkv = _mem_kv(memf, row(g_mem_norm[l]), w_cross_kv, l, tm=memf.shape[0] // batch)
        xf = _merge_cross(xf, oa, ob, z, 3 * tn // (d // 2), w_oa_b, w_ob_b, w_out_b, row(g_cross_norm[l]), w_cq_b, mkv,
                          w_co_b, tm=512, batch=batch)
        xf = _conv_ffn(xf, row(g_ffn_norm[l]), w_up_b, conv_w[l], row(conv_b[l]),
                       w_down_b, row(g_final), tm=512, tf=512, batch=batch,
                       final_norm=(l == depth - 1))
    return xf.reshape(batch, seq, d)
```

```python
import functools
import math

import jax
import jax.numpy as jnp
from jax import lax
from jax.experimental import pallas as pl
from jax.experimental.pallas import tpu as pltpu

F32 = jnp.float32
BF16 = jnp.bfloat16

EPS = 1e-6
LOG2_E = math.log2(math.e)
ROPE_THETA = 10000.0
MLA_HEADS = 8
MLA_NOPE_DIM = 128
MLA_ROPE_DIM = 64
MLA_V_DIM = 128
MLA_QK_PAD = 256
DIFF_HEADS = 8
DIFF_HEAD_DIM = 64
DIFF_V_DIM = 2 * DIFF_HEAD_DIM
XATTN_HEADS = 4
XATTN_HEAD_DIM = 128
LANES = 128
BF16_SUBLANES = 16
MXU_COLS = 256
VMEM_LIMIT_BYTES = 56 * 1024 * 1024

ROWS_MLA_PROJ = 512
ROWS_IN_PROJ = 2048
COLS_IN_PROJ = 1024
Q_CHUNK = 512
ROWS_MERGE = 512
ROWS_FFN = 512
COLS_FFN = 512

_NT = (((1,), (1,)), ((), ()))


def _params(*semantics):
    return pltpu.CompilerParams(dimension_semantics=semantics, vmem_limit_bytes=VMEM_LIMIT_BYTES)


def _rms(x, g):
    ms = jnp.mean(x * x, axis=-1, keepdims=True)
    return x * lax.rsqrt(ms + EPS) * g


def _sigmoid(x):
    return 1.0 / (1.0 + jnp.exp(-x))


def _rope128(x, cos_t, sin_t):
    lane = lax.broadcasted_iota(jnp.int32, x.shape, 1)
    first_half = (lane & (MLA_ROPE_DIM // 2)) == 0
    half = MLA_ROPE_DIM // 2
    partner = jnp.where(first_half, pltpu.roll(x, LANES - half, 1), pltpu.roll(x, half, 1))
    return x * cos_t + partner * sin_t


def _softmax_parts(s):
    m = jnp.max(s, axis=-1, keepdims=True)
    e = jnp.exp2(s - m)
    return e, jnp.sum(e, axis=-1, keepdims=True)


def _mla_proj_kernel(x_ref, g_ref, w1t_ref, gq_ref, gkv_ref, wq_ref, wkv_ref, cos_ref, sin_ref,
                     q_ref, k_ref, v_ref, xn_ref, w1_ref, *, q_rank, kv_rank, q_scale):
    @pl.when(pl.program_id(0) == 0)
    def _():
        w1_ref[...] = w1t_ref[...].astype(BF16)

    xn = _rms(x_ref[...], g_ref[...]).astype(BF16)
    xn_ref[...] = xn
    z = lax.dot_general(xn, w1_ref[...], _NT, preferred_element_type=F32)
    cqn = _rms(z[:, :q_rank], gq_ref[...]).astype(BF16)
    ckvn = _rms(z[:, q_rank:q_rank + kv_rank], gkv_ref[...]).astype(BF16)
    cos_t, sin_t = cos_ref[...], sin_ref[...]
    kpe = _rope128(z[:, q_rank + kv_rank:], cos_t, sin_t).astype(BF16)
    q = jnp.dot(cqn, wq_ref[...], preferred_element_type=F32) * q_scale
    kv = jnp.dot(ckvn, wkv_ref[...], preferred_element_type=F32)
    kv_w = MLA_NOPE_DIM + MLA_V_DIM
    for h in range(MLA_HEADS):
        lo, mid, hi = h * MLA_QK_PAD, h * MLA_QK_PAD + MLA_NOPE_DIM, (h + 1) * MLA_QK_PAD
        q_ref[:, lo:mid] = q[:, lo:mid].astype(BF16)
        q_ref[:, mid:hi] = _rope128(q[:, mid:hi], cos_t, sin_t).astype(BF16)
        k_ref[:, lo:mid] = kv[:, kv_w * h:kv_w * h + MLA_NOPE_DIM].astype(BF16)
        k_ref[:, mid:hi] = kpe
        v_ref[:, h * MLA_V_DIM:(h + 1) * MLA_V_DIM] = kv[:, kv_w * h + MLA_NOPE_DIM:kv_w * (h + 1)].astype(BF16)


def _mla_proj(x, g, w_t, layer, n1, gq, gkv, wq, wkv, cos_t, sin_t, *, tm):
    t, d = x.shape
    q_rank, kv_rank = gq.shape[1], gkv.shape[1]
    full = lambda a: pl.BlockSpec(a.shape, lambda i: (0, 0))
    rows = lambda width: pl.BlockSpec((tm, width), lambda i: (i, 0))
    qk_w, v_w = MLA_HEADS * MLA_QK_PAD, MLA_HEADS * MLA_V_DIM
    kern = functools.partial(_mla_proj_kernel, q_rank=q_rank, kv_rank=kv_rank,
                             q_scale=LOG2_E * (MLA_NOPE_DIM + MLA_ROPE_DIM) ** -0.5)
    return pl.pallas_call(
        kern,
        grid=(t // tm,),
        in_specs=[rows(d), full(g),
                  pl.BlockSpec((None, n1, d), lambda i: (layer, 0, 0), pipeline_mode=pl.Buffered(1)),
                  full(gq), full(gkv), full(wq), full(wkv), rows(LANES), rows(LANES)],
        out_specs=[rows(qk_w), rows(qk_w), rows(v_w), rows(d)],
        out_shape=[jax.ShapeDtypeStruct((t, qk_w), BF16), jax.ShapeDtypeStruct((t, qk_w), BF16),
                   jax.ShapeDtypeStruct((t, v_w), BF16), jax.ShapeDtypeStruct((t, d), BF16)],
        scratch_shapes=[pltpu.VMEM((n1, d), BF16)],
        compiler_params=_params("arbitrary"),
        name="mla_proj",
    )(x, g, w_t, gq, gkv, wq, wkv, cos_t, sin_t)


def _in_proj_kernel(xn_ref, wt_ref, o_ref, wbf_ref):
    @pl.when(pl.program_id(1) == 0)
    def _():
        wbf_ref[...] = wt_ref[0].astype(BF16)

    for c in range(0, o_ref.shape[1], MXU_COLS):
        z = lax.dot_general(xn_ref[...], wbf_ref[c:c + MXU_COLS, :], _NT, preferred_element_type=F32)
        o_ref[:, c:c + MXU_COLS] = z.astype(BF16)


def _in_proj(xn, w_t, layer, *, tm, tn, first_row, n_blocks):
    t, d = xn.shape
    w_spec = pl.BlockSpec((pl.Element(1), pl.Element(tn), pl.Element(d)),
                          lambda j, i: (layer, pl.multiple_of(first_row + j * tn, BF16_SUBLANES), 0))
    return pl.pallas_call(
        _in_proj_kernel,
        grid=(n_blocks, t // tm),
        in_specs=[pl.BlockSpec((tm, d), lambda j, i: (i, 0)), w_spec],
        out_specs=pl.BlockSpec((tm, tn), lambda j, i: (i, j)),
        out_shape=jax.ShapeDtypeStruct((t, n_blocks * tn), BF16),
        scratch_shapes=[pltpu.VMEM((tn, d), BF16)],
        compiler_params=_params("arbitrary", "arbitrary"),
        name="in_proj",
    )(xn, w_t)


def _cast_specs(weights, layer, steps, step_index):
    ins, outs, shapes = [], [], []
    for w in weights:
        _, r, c = w.shape
        rb = r // steps
        assert rb * steps == r and rb % BF16_SUBLANES == 0
        ins.append(pl.BlockSpec((None, rb, c), lambda *g: (layer, step_index(*g), 0)))
        outs.append(pl.BlockSpec((rb, c), lambda *g: (step_index(*g), 0)))
        shapes.append(jax.ShapeDtypeStruct((r, c), BF16))
    return ins, outs, shapes


def _run_casts(src_refs, dst_refs):
    for src, dst in zip(src_refs, dst_refs):
        dst[...] = src[...].astype(BF16)


def _rows(c, tq):
    if isinstance(c, int):
        return slice(c * tq, (c + 1) * tq)
    return pl.ds(pl.multiple_of(c * tq, tq), tq)


def _pipelined_chunks(n, qk, sm, pv):
    def step(c, parity):
        qk(c, parity)
        sm(1 - parity)
        pv(c - 2, parity)

    qk(0, 0)
    qk(1, 1)
    sm(0)

    def body(i, carry):
        step(2 * i, 0)
        step(2 * i + 1, 1)
        return carry

    lax.fori_loop(1, n // 2, body, 0)
    sm(1)
    pv(n - 2, 0)
    pv(n - 1, 1)


def _fill_v_ones(va_ref, v_ref):
    w = v_ref.shape[1]
    va_ref[:, :w] = v_ref[...]
    va_ref[:, w:] = jnp.ones((va_ref.shape[0], va_ref.shape[1] - w), BF16)


def _exp_scores(s):
    return jnp.exp2(s - jnp.max(s, axis=-1, keepdims=True)).astype(BF16)


def _mla_attn_kernel(q_ref, k_ref, v_ref, *refs, tq, n_cast):
    o_ref = refs[n_cast]
    s0, s1, p0, p1, va_ref = refs[2 * n_cast + 1:]
    _run_casts(refs[:n_cast], refs[n_cast + 1:2 * n_cast + 1])
    sb, pb = (s0, s1), (p0, p1)
    dv = v_ref.shape[1]
    _fill_v_ones(va_ref, v_ref)

    def qk(c, slot):
        sb[slot][...] = lax.dot_general(q_ref[_rows(c, tq), :], k_ref[...], _NT, preferred_element_type=F32)

    def sm(slot):
        pb[slot][...] = _exp_scores(sb[slot][...])

    def pv(c, slot):
        o = jnp.dot(pb[slot][...], va_ref[...], preferred_element_type=F32)
        o_ref[_rows(c, tq), :] = (o[:, :dv] * (1.0 / o[:, dv:])).astype(BF16)

    _pipelined_chunks(q_ref.shape[0] // tq, qk, sm, pv)


def _mla_attn(q, k, v, cast_weights, layer, *, batch, tq):
    t = q.shape[0]
    s = t // batch
    c_in, c_out, c_shapes = _cast_specs(cast_weights, layer, batch * MLA_HEADS, lambda b, h: b * MLA_HEADS + h)
    return pl.pallas_call(
        functools.partial(_mla_attn_kernel, tq=tq, n_cast=len(cast_weights)),
        grid=(batch, MLA_HEADS),
        in_specs=[pl.BlockSpec((s, MLA_QK_PAD), lambda b, h: (b, h)),
                  pl.BlockSpec((s, MLA_QK_PAD), lambda b, h: (b, h)),
                  pl.BlockSpec((s, MLA_V_DIM), lambda b, h: (b, h))] + c_in,
        out_specs=[pl.BlockSpec((s, MLA_V_DIM), lambda b, h: (b, h))] + c_out,
        out_shape=[jax.ShapeDtypeStruct((t, MLA_HEADS * MLA_V_DIM), BF16)] + c_shapes,
        scratch_shapes=[pltpu.VMEM((tq, s), F32), pltpu.VMEM((tq, s), F32),
                        pltpu.VMEM((tq, s), BF16), pltpu.VMEM((tq, s), BF16),
                        pltpu.VMEM((s, 2 * MLA_V_DIM), BF16)],
        compiler_params=_params("parallel", "parallel"),
        name="mla_attn",
    )(q, k, v, *cast_weights)


def _diff_attn_kernel(lq1_ref, lk1_ref, lq2_ref, lk2_ref, gsub_ref, cos_ref, sin_ref, q_ref, k_ref, v_ref,
                      *refs, tq, lam_init, q_scale, n_cast):
    o_ref = refs[n_cast]
    s0, s1, p0, p1, va_ref, kr_ref = refs[2 * n_cast + 1:]
    _run_casts(refs[:n_cast], refs[n_cast + 1:2 * n_cast + 1])
    lam = (jnp.exp(jnp.sum(lq1_ref[...] * lk1_ref[...], axis=-1, keepdims=True))
           - jnp.exp(jnp.sum(lq2_ref[...] * lk2_ref[...], axis=-1, keepdims=True)) + lam_init)
    gain = gsub_ref[...] * (1.0 - lam_init)
    lane = lax.broadcasted_iota(jnp.int32, (tq, 2 * DIFF_HEAD_DIM), 1)
    map0 = lane < DIFF_HEAD_DIM
    sb, pb = (s0, s1), (p0, p1)
    dv = v_ref.shape[1]
    _fill_v_ones(va_ref, v_ref)
    kr_ref[...] = _rope128(k_ref[...].astype(F32), cos_ref[...], sin_ref[...]).astype(BF16)

    def qk(c, slot):
        rows = _rows(c, tq)
        q = _rope128(q_ref[rows, :].astype(F32), cos_ref[rows, :], sin_ref[rows, :]) * q_scale
        q = q.astype(BF16)
        zero = jnp.zeros_like(q)
        q01 = jnp.concatenate([jnp.where(map0, q, zero), jnp.where(map0, zero, q)], axis=0)
        sb[slot][...] = lax.dot_general(q01, kr_ref[...], _NT, preferred_element_type=F32)

    def sm(slot):
        pb[slot][...] = _exp_scores(sb[slot][...])

    def pv(c, slot):
        o = jnp.dot(pb[slot][...], va_ref[...], preferred_element_type=F32)
        o = o[:, :dv] * (1.0 / o[:, dv:])
        o_ref[_rows(c, tq), :] = _rms(o[:tq] - lam * o[tq:], gain).astype(BF16)

    _pipelined_chunks(q_ref.shape[0] // tq, qk, sm, pv)


def _diff_attn(lq1, lk1, lq2, lk2, gsub, cos_t, sin_t, z, cast_weights, layer,
               *, batch, tq, q_col, k_col, v_col, lam_init):
    t = z.shape[0]
    s = t // batch
    hw = 2 * DIFF_HEAD_DIM
    small = lambda a: pl.BlockSpec(a.shape, lambda b, h: (0, 0))
    head = lambda col: pl.BlockSpec((s, hw), lambda b, h: (b, col + h))
    c_in, c_out, c_shapes = _cast_specs(cast_weights, layer, batch * DIFF_HEADS, lambda b, h: b * DIFF_HEADS + h)
    return pl.pallas_call(
        functools.partial(_diff_attn_kernel, tq=tq, lam_init=lam_init, n_cast=len(cast_weights),
                          q_scale=LOG2_E * DIFF_HEAD_DIM ** -0.5),
        grid=(batch, DIFF_HEADS),
        in_specs=[small(lq1), small(lk1), small(lq2), small(lk2), small(gsub),
                  pl.BlockSpec((s, LANES), lambda b, h: (b, 0)), pl.BlockSpec((s, LANES), lambda b, h: (b, 0)),
                  head(q_col), head(k_col), head(v_col)] + c_in,
        out_specs=[pl.BlockSpec((s, DIFF_V_DIM), lambda b, h: (b, h))] + c_out,
        out_shape=[jax.ShapeDtypeStruct((t, DIFF_HEADS * DIFF_V_DIM), BF16)] + c_shapes,
        scratch_shapes=[pltpu.VMEM((2 * tq, s), F32), pltpu.VMEM((2 * tq, s), F32),
                        pltpu.VMEM((2 * tq, s), BF16), pltpu.VMEM((2 * tq, s), BF16),
                        pltpu.VMEM((s, 2 * DIFF_V_DIM), BF16), pltpu.VMEM((s, hw), BF16)],
        compiler_params=_params("parallel", "parallel"),
        name="diff_attn",
    )(lq1, lk1, lq2, lk2, gsub, cos_t, sin_t, z, z, z, *cast_weights)


def _mem_kv_kernel(m_ref, g_ref, w_ref, o_ref, wbf_ref):
    @pl.when(pl.program_id(0) == 0)
    def _():
        wbf_ref[...] = w_ref[...].astype(BF16)

    mn = _rms(m_ref[...], g_ref[...]).astype(BF16)
    o_ref[...] = jnp.dot(mn, wbf_ref[...], preferred_element_type=F32).astype(BF16)


def _mem_kv(mem, g, w, layer, *, tm):
    t, d = mem.shape
    n = w.shape[2]
    return pl.pallas_call(
        _mem_kv_kernel,
        grid=(t // tm,),
        in_specs=[pl.BlockSpec((tm, d), lambda i: (i, 0)),
                  pl.BlockSpec((1, d), lambda i: (0, 0)),
                  pl.BlockSpec((None, d, n), lambda i: (layer, 0, 0), pipeline_mode=pl.Buffered(1))],
        out_specs=pl.BlockSpec((tm, n), lambda i: (i, 0)),
        out_shape=jax.ShapeDtypeStruct((t, n), BF16),
        scratch_shapes=[pltpu.VMEM((d, n), BF16)],
        compiler_params=_params("arbitrary"),
        name="mem_kv",
    )(mem, g, w)


def _merge_cross_kernel(x_ref, oa_ref, ob_ref, ga0_ref, ga1_ref, gb0_ref, gb1_ref, woa_ref, wob_ref, wout_ref,
                        gc_ref, wcq_ref, mkv_ref, wco_ref, o_ref, *, q_scale):
    ya = jnp.dot(oa_ref[...], woa_ref[...], preferred_element_type=F32)
    yb = jnp.dot(ob_ref[...], wob_ref[...], preferred_element_type=F32)
    ga = _sigmoid(jnp.concatenate([ga0_ref[...], ga1_ref[...]], axis=1).astype(F32))
    gb = _sigmoid(jnp.concatenate([gb0_ref[...], gb1_ref[...]], axis=1).astype(F32))
    merged = ga * ya + gb * yb
    x1 = x_ref[...] + jnp.dot(merged.astype(BF16), wout_ref[...], preferred_element_type=F32)

    hn = _rms(x1, gc_ref[...]).astype(BF16)
    qc = (jnp.dot(hn, wcq_ref[...], preferred_element_type=F32) * q_scale).astype(BF16)
    hd = XATTN_HEAD_DIM
    heads = []
    for h in range(XATTN_HEADS):
        k_h = mkv_ref[:, 2 * h * hd:(2 * h + 1) * hd]
        v_h = mkv_ref[:, (2 * h + 1) * hd:(2 * h + 2) * hd]
        s = lax.dot_general(qc[:, h * hd:(h + 1) * hd], k_h, _NT, preferred_element_type=F32)
        e, l = _softmax_parts(s)
        o_h = jnp.dot(e.astype(BF16), v_h, preferred_element_type=F32) * (1.0 / l)
        heads.append(o_h.astype(BF16))
    oc = jnp.concatenate(heads, axis=1)
    o_ref[...] = x1 + jnp.dot(oc, wco_ref[...], preferred_element_type=F32)


def _merge_cross(x, oa, ob, z, gate_block, woa, wob, wout, gc, wcq, mkv, wco, *, tm, batch):
    t, d = x.shape
    blocks_per_seq = t // batch // tm
    m = mkv.shape[0] // batch
    const = lambda a: pl.BlockSpec(a.shape, lambda i: (0, 0), pipeline_mode=pl.Buffered(1))
    rows = lambda width: pl.BlockSpec((tm, width), lambda i: (i, 0))
    return pl.pallas_call(
        functools.partial(_merge_cross_kernel, q_scale=LOG2_E * XATTN_HEAD_DIM ** -0.5),
        grid=(t // tm,),
        in_specs=[rows(d), rows(oa.shape[1]), rows(ob.shape[1])]
                 + [pl.BlockSpec((tm, d // 2), lambda i, k=k: (i, gate_block + k)) for k in range(4)]
                 + [const(woa), const(wob), const(wout), const(gc), const(wcq),
                  pl.BlockSpec((m, mkv.shape[1]), lambda i: (i // blocks_per_seq, 0)),
                  const(wco)],
        out_specs=rows(d),
        out_shape=jax.ShapeDtypeStruct((t, d), F32),
        compiler_params=_params("parallel"),
        name="merge_cross",
    )(x, oa, ob, z, z, z, z, woa, wob, wout, gc, wcq, mkv, wco)


def _conv_ffn_kernel(xp_ref, x_ref, xn_ref, g_ref, wga_ref, wgb_ref, wva_ref, wvb_ref, c_ref, wda_ref, wdb_ref,
                     gfin_ref, o_ref, h_ref, uga_ref, uva_ref, ugb_ref, uvb_ref, acc_ref,
                     *, tm, halo, blocks_per_seq, n_chunks, final_norm):
    i, f = pl.program_id(0), pl.program_id(1)
    last = pl.num_programs(1) - 1

    @pl.when(f == 0)
    def _():
        g = g_ref[...]
        pos = i % blocks_per_seq
        keep_prev = jnp.where(pos == 0, 0.0, 1.0)
        keep_next = jnp.where(pos == blocks_per_seq - 1, 0.0, 1.0)
        h_ref[0:halo, :] = (_rms(xp_ref[...], g) * keep_prev).astype(BF16)
        h_ref[halo:halo + tm, :] = _rms(x_ref[...], g).astype(BF16)
        h_ref[halo + tm:, :] = (_rms(xn_ref[...], g) * keep_next).astype(BF16)
        acc_ref[...] = jnp.zeros_like(acc_ref)

    def conv(u_ref, c):
        return (u_ref[halo - 1:halo - 1 + tm, :] * c[0:1, :] + u_ref[halo:halo + tm, :] * c[1:2, :]
                + u_ref[halo + 1:halo + 1 + tm, :] * c[2:3, :] + c[3:4, :])

    def chunk(idx, wg_ref, wv_ref, wd_ref, ug_ref, uv_ref):
        h = h_ref[...]
        ug_ref[...] = jnp.dot(h, wg_ref[...], preferred_element_type=F32)
        uv_ref[...] = jnp.dot(h, wv_ref[...], preferred_element_type=F32)
        yg = conv(ug_ref, c_ref[idx])
        yv = conv(uv_ref, c_ref[n_chunks + idx])
        act = (yg * _sigmoid(yg) * yv).astype(BF16)
        return jnp.dot(act, wd_ref[...], preferred_element_type=F32)

    @pl.when(f < last)
    def _():
        acc_ref[...] += (chunk(2 * f, wga_ref, wva_ref, wda_ref, uga_ref, uva_ref)
                         + chunk(2 * f + 1, wgb_ref, wvb_ref, wdb_ref, ugb_ref, uvb_ref))

    @pl.when(f == last)
    def _():
        y = x_ref[...] + acc_ref[...] + chunk(2 * f, wga_ref, wva_ref, wda_ref, uga_ref, uva_ref)
        o_ref[...] = _rms(y, gfin_ref[...]) if final_norm else y


def _conv_ffn(x, g, w_up, conv_w, conv_b, w_down, g_final, *, tm, tf, batch, final_norm):
    t, d = x.shape
    ffn = w_down.shape[0]
    n_chunks = ffn // tf
    assert n_chunks % 2 == 1
    steps = n_chunks // 2 + 1
    halo = BF16_SUBLANES
    blocks_per_seq = t // batch // tm
    hb = tm // halo
    last_halo_block = t // halo - 1
    kern = functools.partial(_conv_ffn_kernel, tm=tm, halo=halo, blocks_per_seq=blocks_per_seq,
                             n_chunks=n_chunks, final_norm=final_norm)
    conv_wb = jnp.concatenate([conv_w, conv_b], axis=0)
    conv_wb = conv_wb.reshape(conv_wb.shape[0], 2 * n_chunks, tf).transpose(1, 0, 2)
    ca = lambda f: 2 * f
    cb = lambda f: jnp.minimum(2 * f + 1, n_chunks - 1)
    return pl.pallas_call(
        kern,
        grid=(t // tm, steps),
        in_specs=[pl.BlockSpec((halo, d), lambda i, f: (jnp.maximum(i * hb - 1, 0), 0)),
                  pl.BlockSpec((tm, d), lambda i, f: (i, 0)),
                  pl.BlockSpec((halo, d), lambda i, f: (jnp.minimum((i + 1) * hb, last_halo_block), 0)),
                  pl.BlockSpec((1, d), lambda i, f: (0, 0)),
                  pl.BlockSpec((d, tf), lambda i, f: (0, ca(f))),
                  pl.BlockSpec((d, tf), lambda i, f: (0, cb(f))),
                  pl.BlockSpec((d, tf), lambda i, f: (0, n_chunks + ca(f))),
                  pl.BlockSpec((d, tf), lambda i, f: (0, n_chunks + cb(f))),
                  pl.BlockSpec(conv_wb.shape, lambda i, f: (0, 0, 0)),
                  pl.BlockSpec((tf, d), lambda i, f: (ca(f), 0)),
                  pl.BlockSpec((tf, d), lambda i, f: (cb(f), 0)),
                  pl.BlockSpec((1, d), lambda i, f: (0, 0))],
        out_specs=pl.BlockSpec((tm, d), lambda i, f: (i, 0)),
        out_shape=jax.ShapeDtypeStruct((t, d), F32),
        scratch_shapes=[pltpu.VMEM((tm + 2 * halo, d), BF16)]
                       + [pltpu.VMEM((tm + 2 * halo, tf), F32)] * 4
                       + [pltpu.VMEM((tm, d), F32)],
        compiler_params=_params("parallel", "arbitrary"),
        name="conv_ffn",
    )(x, x, x, g, w_up, w_up, w_up, w_up, conv_wb, w_down, w_down, g_final)


def _rope_tables(positions):
    dim = MLA_ROPE_DIM
    inv = ROPE_THETA ** (-jnp.arange(0, dim, 2, dtype=F32) / dim)
    ang = positions.astype(F32)[..., None] * inv
    cos, sin = jnp.cos(ang), jnp.sin(ang)
    reps = LANES // dim
    cos_t = jnp.concatenate([cos, cos] * reps, axis=-1)
    sin_t = jnp.concatenate([-sin, sin] * reps, axis=-1)
    return cos_t.reshape(-1, LANES), sin_t.reshape(-1, LANES)


def kernel(x, mem, positions, g_mix_norm, w_in, g_q_norm, w_uq, g_kv_norm, w_ukv, w_o_mla, lambda_q1, lambda_k1, lambda_q2, lambda_k2, g_diff_sub, w_o_diff, w_out, g_cross_norm, g_mem_norm, w_cross_q, w_cross_kv, w_cross_o, g_ffn_norm, w_up, conv_w, conv_b, w_down, g_final):
    batch, seq, d = x.shape
    depth = w_in.shape[0]
    t = batch * seq
    assert depth >= 1 and MLA_ROPE_DIM == DIFF_HEAD_DIM and MLA_NOPE_DIM == MLA_V_DIM
    q_rank, kv_rank = g_q_norm.shape[1], g_kv_norm.shape[1]
    dqk = DIFF_HEADS * 2 * DIFF_HEAD_DIM
    dv_w = DIFF_HEADS * DIFF_V_DIM
    c0 = q_rank + kv_rank + MLA_ROPE_DIM
    tn = COLS_IN_PROJ
    assert dqk == tn and dv_w == tn and (2 * d) % tn == 0

    cos_t, sin_t = _rope_tables(positions)
    xf = x.reshape(t, d)
    memf = mem.reshape(-1, d)
    row = lambda v: v.reshape(1, -1)
    w_in_t = jnp.swapaxes(w_in, 1, 2)

    for l in range(depth):
        lam_init = 0.8 - 0.6 * math.exp(-0.3 * l)
        wq = jnp.pad(w_uq[l].reshape(q_rank, MLA_HEADS, MLA_NOPE_DIM + MLA_ROPE_DIM),
                     ((0, 0), (0, 0), (0, MLA_QK_PAD - MLA_NOPE_DIM - MLA_ROPE_DIM)))
        wq = wq.reshape(q_rank, MLA_HEADS * MLA_QK_PAD).astype(BF16)
        wkv = w_ukv[l].astype(BF16)

        q, k, v, xn = _mla_proj(xf, row(g_mix_norm[l]), w_in_t, l, c0 + LANES - MLA_ROPE_DIM,
                                row(g_q_norm[l]), row(g_kv_norm[l]), wq, wkv, cos_t, sin_t, tm=ROWS_MLA_PROJ)
        z = _in_proj(xn, w_in_t, l, tm=ROWS_IN_PROJ, tn=tn, first_row=c0, n_blocks=3 + 2 * d // tn)
        oa, w_up_b, w_down_b = _mla_attn(q, k, v, [w_up, w_down], l, batch=batch, tq=Q_CHUNK)
        ob, w_oa_b, w_ob_b, w_out_b, w_cq_b, w_co_b = _diff_attn(
            row(lambda_q1[l]), row(lambda_k1[l]), row(lambda_q2[l]), row(lambda_k2[l]), row(g_diff_sub[l]),
            cos_t, sin_t, z, [w_o_mla, w_o_diff, w_out, w_cross_q, w_cross_o], l, batch=batch, tq=Q_CHUNK,
            q_col=0, k_col=DIFF_HEADS, v_col=2 * DIFF_HEADS, lam_init=lam_init)
        mkv = _mem_kv(memf, row(g_mem_norm[l]), w_cross_kv, l, tm=memf.shape[0] // batch)
        xf = _merge_cross(xf, oa, ob, z, 3 * tn // (d // 2), w_oa_b, w_ob_b, w_out_b, row(g_cross_norm[l]), w_cq_b, mkv,
                          w_co_b, tm=ROWS_MERGE, batch=batch)
        xf = _conv_ffn(xf, row(g_ffn_norm[l]), w_up_b, conv_w[l], row(conv_b[l]),
                       w_down_b, row(g_final), tm=ROWS_FFN, tf=COLS_FFN, batch=batch,
                       final_norm=(l == depth - 1))
    return xf.reshape(batch, seq, d)
```

```python
import functools
import math

import jax
import jax.numpy as jnp
from jax import lax
from jax.experimental import pallas as pl
from jax.experimental.pallas import tpu as pltpu

F32 = jnp.float32
BF16 = jnp.bfloat16

EPS = 1e-6
LOG2_E = math.log2(math.e)
ROPE_THETA = 10000.0
MLA_HEADS = 8
MLA_NOPE_DIM = 128
MLA_ROPE_DIM = 64
MLA_V_DIM = 128
MLA_QK_PAD = 256
DIFF_HEADS = 8
DIFF_HEAD_DIM = 64
DIFF_V_DIM = 2 * DIFF_HEAD_DIM
XATTN_HEADS = 4
XATTN_HEAD_DIM = 128
LANES = 128
BF16_SUBLANES = 16
MXU_COLS = 256
VMEM_LIMIT_BYTES = 56 * 1024 * 1024

ROWS_MLA_PROJ = 512
ROWS_IN_PROJ = 2048
COLS_IN_PROJ = 1024
Q_CHUNK = 512
ROWS_MERGE = 512
ROWS_FFN = 512
COLS_FFN = 512

_NT = (((1,), (1,)), ((), ()))


def _params(*semantics):
    return pltpu.CompilerParams(dimension_semantics=semantics, vmem_limit_bytes=VMEM_LIMIT_BYTES)


def _rms(x, g):
    ms = jnp.mean(x * x, axis=-1, keepdims=True)
    return x * lax.rsqrt(ms + EPS) * g


def _sigmoid(x):
    return 1.0 / (1.0 + jnp.exp(-x))


def _rope128(x, cos_t, sin_t):
    lane = lax.broadcasted_iota(jnp.int32, x.shape, 1)
    first_half = (lane & (MLA_ROPE_DIM // 2)) == 0
    half = MLA_ROPE_DIM // 2
    partner = jnp.where(first_half, pltpu.roll(x, LANES - half, 1), pltpu.roll(x, half, 1))
    return x * cos_t + partner * sin_t


def _softmax_parts(s):
    m = jnp.max(s, axis=-1, keepdims=True)
    e = jnp.exp2(s - m)
    return e, jnp.sum(e, axis=-1, keepdims=True)


def _mla_proj_kernel(x_ref, g_ref, w1t_ref, gq_ref, gkv_ref, wq_ref, wkv_ref, cos_ref, sin_ref,
                     q_ref, k_ref, v_ref, xn_ref, w1_ref, *, q_rank, kv_rank, q_scale):
    @pl.when(pl.program_id(0) == 0)
    def _():
        w1_ref[...] = w1t_ref[...].astype(BF16)

    xn = _rms(x_ref[...], g_ref[...]).astype(BF16)
    xn_ref[...] = xn
    z = lax.dot_general(xn, w1_ref[...], _NT, preferred_element_type=F32)
    cqn = _rms(z[:, :q_rank], gq_ref[...]).astype(BF16)
    ckvn = _rms(z[:, q_rank:q_rank + kv_rank], gkv_ref[...]).astype(BF16)
    cos_t, sin_t = cos_ref[...], sin_ref[...]
    kpe = _rope128(z[:, q_rank + kv_rank:], cos_t, sin_t).astype(BF16)
    q = jnp.dot(cqn, wq_ref[...], preferred_element_type=F32) * q_scale
    kv = jnp.dot(ckvn, wkv_ref[...], preferred_element_type=F32)
    kv_w = MLA_NOPE_DIM + MLA_V_DIM
    for h in range(MLA_HEADS):
        lo, mid, hi = h * MLA_QK_PAD, h * MLA_QK_PAD + MLA_NOPE_DIM, (h + 1) * MLA_QK_PAD
        q_ref[:, lo:mid] = q[:, lo:mid].astype(BF16)
        q_ref[:, mid:hi] = _rope128(q[:, mid:hi], cos_t, sin_t).astype(BF16)
        k_ref[:, lo:mid] = kv[:, kv_w * h:kv_w * h + MLA_NOPE_DIM].astype(BF16)
        k_ref[:, mid:hi] = kpe
        v_ref[:, h * MLA_V_DIM:(h + 1) * MLA_V_DIM] = kv[:, kv_w * h + MLA_NOPE_DIM:kv_w * (h + 1)].astype(BF16)


def _mla_proj(x, g, w_t, layer, n1, gq, gkv, wq, wkv, cos_t, sin_t, *, tm):
    t, d = x.shape
    q_rank, kv_rank = gq.shape[1], gkv.shape[1]
    full = lambda a: pl.BlockSpec(a.shape, lambda i: (0, 0))
    rows = lambda width: pl.BlockSpec((tm, width), lambda i: (i, 0))
    qk_w, v_w = MLA_HEADS * MLA_QK_PAD, MLA_HEADS * MLA_V_DIM
    kern = functools.partial(_mla_proj_kernel, q_rank=q_rank, kv_rank=kv_rank,
                             q_scale=LOG2_E * (MLA_NOPE_DIM + MLA_ROPE_DIM) ** -0.5)
    return pl.pallas_call(
        kern,
        grid=(t // tm,),
        in_specs=[rows(d), full(g),
                  pl.BlockSpec((None, n1, d), lambda i: (layer, 0, 0), pipeline_mode=pl.Buffered(1)),
                  full(gq), full(gkv), full(wq), full(wkv), rows(LANES), rows(LANES)],
        out_specs=[rows(qk_w), rows(qk_w), rows(v_w), rows(d)],
        out_shape=[jax.ShapeDtypeStruct((t, qk_w), BF16), jax.ShapeDtypeStruct((t, qk_w), BF16),
                   jax.ShapeDtypeStruct((t, v_w), BF16), jax.ShapeDtypeStruct((t, d), BF16)],
        scratch_shapes=[pltpu.VMEM((n1, d), BF16)],
        compiler_params=_params("arbitrary"),
        name="mla_proj",
    )(x, g, w_t, gq, gkv, wq, wkv, cos_t, sin_t)


def _in_proj_kernel(xn_ref, wt_ref, o_ref, wbf_ref):
    @pl.when(pl.program_id(1) == 0)
    def _():
        wbf_ref[...] = wt_ref[0].astype(BF16)

    for c in range(0, o_ref.shape[1], MXU_COLS):
        z = lax.dot_general(xn_ref[...], wbf_ref[c:c + MXU_COLS, :], _NT, preferred_element_type=F32)
        o_ref[:, c:c + MXU_COLS] = z.astype(BF16)


def _in_proj(xn, w_t, layer, *, tm, tn, first_row, n_blocks):
    t, d = xn.shape
    w_spec = pl.BlockSpec((pl.Element(1), pl.Element(tn), pl.Element(d)),
                          lambda j, i: (layer, pl.multiple_of(first_row + j * tn, BF16_SUBLANES), 0))
    return pl.pallas_call(
        _in_proj_kernel,
        grid=(n_blocks, t // tm),
        in_specs=[pl.BlockSpec((tm, d), lambda j, i: (i, 0)), w_spec],
        out_specs=pl.BlockSpec((tm, tn), lambda j, i: (i, j)),
        out_shape=jax.ShapeDtypeStruct((t, n_blocks * tn), BF16),
        scratch_shapes=[pltpu.VMEM((tn, d), BF16)],
        compiler_params=_params("arbitrary", "arbitrary"),
        name="in_proj",
    )(xn, w_t)


def _cast_specs(weights, layer, steps, step_index):
    ins, outs, shapes = [], [], []
    for w in weights:
        _, r, c = w.shape
        rb = r // steps
        assert rb * steps == r and rb % BF16_SUBLANES == 0
        ins.append(pl.BlockSpec((None, rb, c), lambda *g: (layer, step_index(*g), 0)))
        outs.append(pl.BlockSpec((rb, c), lambda *g: (step_index(*g), 0)))
        shapes.append(jax.ShapeDtypeStruct((r, c), BF16))
    return ins, outs, shapes


def _run_casts(src_refs, dst_refs):
    for src, dst in zip(src_refs, dst_refs):
        dst[...] = src[...].astype(BF16)


def _rows(c, tq):
    if isinstance(c, int):
        return slice(c * tq, (c + 1) * tq)
    return pl.ds(pl.multiple_of(c * tq, tq), tq)


def _pipelined_chunks(n, qk, sm, pv):
    def step(c, parity):
        qk(c, parity)
        sm(1 - parity)
        pv(c - 2, parity)

    qk(0, 0)
    qk(1, 1)
    sm(0)

    def body(i, carry):
        step(2 * i, 0)
        step(2 * i + 1, 1)
        return carry

    lax.fori_loop(1, n // 2, body, 0)
    sm(1)
    pv(n - 2, 0)
    pv(n - 1, 1)


def _fill_v_ones(va_ref, v_ref):
    w = v_ref.shape[1]
    va_ref[:, :w] = v_ref[...]
    va_ref[:, w:] = jnp.ones((va_ref.shape[0], va_ref.shape[1] - w), BF16)


def _exp_scores(s):
    return jnp.exp2(s - jnp.max(s, axis=-1, keepdims=True)).astype(BF16)


def _mla_attn_kernel(q_ref, k_ref, v_ref, *refs, tq, n_cast):
    o_ref = refs[n_cast]
    s0, s1, p0, p1, va_ref = refs[2 * n_cast + 1:]
    _run_casts(refs[:n_cast], refs[n_cast + 1:2 * n_cast + 1])
    sb, pb = (s0, s1), (p0, p1)
    dv = v_ref.shape[1]
    _fill_v_ones(va_ref, v_ref)

    def qk(c, slot):
        sb[slot][...] = lax.dot_general(q_ref[_rows(c, tq), :], k_ref[...], _NT, preferred_element_type=F32)

    def sm(slot):
        pb[slot][...] = _exp_scores(sb[slot][...])

    def pv(c, slot):
        o = jnp.dot(pb[slot][...], va_ref[...], preferred_element_type=F32)
        o_ref[_rows(c, tq), :] = (o[:, :dv] * (1.0 / o[:, dv:])).astype(BF16)

    _pipelined_chunks(q_ref.shape[0] // tq, qk, sm, pv)


def _mla_attn(q, k, v, cast_weights, layer, *, batch, tq):
    t = q.shape[0]
    s = t // batch
    c_in, c_out, c_shapes = _cast_specs(cast_weights, layer, batch * MLA_HEADS, lambda b, h: b * MLA_HEADS + h)
    return pl.pallas_call(
        functools.partial(_mla_attn_kernel, tq=tq, n_cast=len(cast_weights)),
        grid=(batch, MLA_HEADS),
        in_specs=[pl.BlockSpec((s, MLA_QK_PAD), lambda b, h: (b, h)),
                  pl.BlockSpec((s, MLA_QK_PAD), lambda b, h: (b, h)),
                  pl.BlockSpec((s, MLA_V_DIM), lambda b, h: (b, h))] + c_in,
        out_specs=[pl.BlockSpec((s, MLA_V_DIM), lambda b, h: (b, h))] + c_out,
        out_shape=[jax.ShapeDtypeStruct((t, MLA_HEADS * MLA_V_DIM), BF16)] + c_shapes,
        scratch_shapes=[pltpu.VMEM((tq, s), F32), pltpu.VMEM((tq, s), F32),
                        pltpu.VMEM((tq, s), BF16), pltpu.VMEM((tq, s), BF16),
                        pltpu.VMEM((s, 2 * MLA_V_DIM), BF16)],
        compiler_params=_params("parallel", "parallel"),
        name="mla_attn",
    )(q, k, v, *cast_weights)


def _diff_attn_kernel(lq1_ref, lk1_ref, lq2_ref, lk2_ref, gsub_ref, cos_ref, sin_ref, q_ref, k_ref, v_ref,
                      *refs, tq, lam_init, q_scale, n_cast):
    o_ref = refs[n_cast]
    s0, s1, p0, p1, va_ref, kr_ref = refs[2 * n_cast + 1:]
    _run_casts(refs[:n_cast], refs[n_cast + 1:2 * n_cast + 1])
    lam = (jnp.exp(jnp.sum(lq1_ref[...] * lk1_ref[...], axis=-1, keepdims=True))
           - jnp.exp(jnp.sum(lq2_ref[...] * lk2_ref[...], axis=-1, keepdims=True)) + lam_init)
    gain = gsub_ref[...] * (1.0 - lam_init)
    lane = lax.broadcasted_iota(jnp.int32, (tq, 2 * DIFF_HEAD_DIM), 1)
    map0 = lane < DIFF_HEAD_DIM
    sb, pb = (s0, s1), (p0, p1)
    dv = v_ref.shape[1]
    _fill_v_ones(va_ref, v_ref)
    kr_ref[...] = _rope128(k_ref[...].astype(F32), cos_ref[...], sin_ref[...]).astype(BF16)

    def qk(c, slot):
        rows = _rows(c, tq)
        q = _rope128(q_ref[rows, :].astype(F32), cos_ref[rows, :], sin_ref[rows, :]) * q_scale
        q = q.astype(BF16)
        zero = jnp.zeros_like(q)
        q01 = jnp.concatenate([jnp.where(map0, q, zero), jnp.where(map0, zero, q)], axis=0)
        sb[slot][...] = lax.dot_general(q01, kr_ref[...], _NT, preferred_element_type=F32)

    def sm(slot):
        pb[slot][...] = _exp_scores(sb[slot][...])

    def pv(c, slot):
        o = jnp.dot(pb[slot][...], va_ref[...], preferred_element_type=F32)
        o = o[:, :dv] * (1.0 / o[:, dv:])
        o_ref[_rows(c, tq), :] = _rms(o[:tq] - lam * o[tq:], gain).astype(BF16)

    _pipelined_chunks(q_ref.shape[0] // tq, qk, sm, pv)


def _diff_attn(lq1, lk1, lq2, lk2, gsub, cos_t, sin_t, z, cast_weights, layer,
               *, batch, tq, q_col, k_col, v_col, lam_init):
    t = z.shape[0]
    s = t // batch
    hw = 2 * DIFF_HEAD_DIM
    small = lambda a: pl.BlockSpec(a.shape, lambda b, h: (0, 0))
    head = lambda col: pl.BlockSpec((s, hw), lambda b, h: (b, col + h))
    c_in, c_out, c_shapes = _cast_specs(cast_weights, layer, batch * DIFF_HEADS, lambda b, h: b * DIFF_HEADS + h)
    return pl.pallas_call(
        functools.partial(_diff_attn_kernel, tq=tq, lam_init=lam_init, n_cast=len(cast_weights),
                          q_scale=LOG2_E * DIFF_HEAD_DIM ** -0.5),
        grid=(batch, DIFF_HEADS),
        in_specs=[small(lq1), small(lk1), small(lq2), small(lk2), small(gsub),
                  pl.BlockSpec((s, LANES), lambda b, h: (b, 0)), pl.BlockSpec((s, LANES), lambda b, h: (b, 0)),
                  head(q_col), head(k_col), head(v_col)] + c_in,
        out_specs=[pl.BlockSpec((s, DIFF_V_DIM), lambda b, h: (b, h))] + c_out,
        out_shape=[jax.ShapeDtypeStruct((t, DIFF_HEADS * DIFF_V_DIM), BF16)] + c_shapes,
        scratch_shapes=[pltpu.VMEM((2 * tq, s), F32), pltpu.VMEM((2 * tq, s), F32),
                        pltpu.VMEM((2 * tq, s), BF16), pltpu.VMEM((2 * tq, s), BF16),
                        pltpu.VMEM((s, 2 * DIFF_V_DIM), BF16), pltpu.VMEM((s, hw), BF16)],
        compiler_params=_params("parallel", "parallel"),
        name="diff_attn",
    )(lq1, lk1, lq2, lk2, gsub, cos_t, sin_t, z, z, z, *cast_weights)


def _mem_kv_kernel(m_ref, g_ref, w_ref, o_ref, wbf_ref):
    @pl.when(pl.program_id(0) == 0)
    def _():
        wbf_ref[...] = w_ref[...].astype(BF16)

    mn = _rms(m_ref[...], g_ref[...]).astype(BF16)
    o_ref[...] = jnp.dot(mn, wbf_ref[...], preferred_element_type=F32).astype(BF16)


def _mem_kv(mem, g, w, layer, *, tm):
    t, d = mem.shape
    n = w.shape[2]
    return pl.pallas_call(
        _mem_kv_kernel,
        grid=(t // tm,),
        in_specs=[pl.BlockSpec((tm, d), lambda i: (i, 0)),
                  pl.BlockSpec((1, d), lambda i: (0, 0)),
                  pl.BlockSpec((None, d, n), lambda i: (layer, 0, 0), pipeline_mode=pl.Buffered(1))],
        out_specs=pl.BlockSpec((tm, n), lambda i: (i, 0)),
        out_shape=jax.ShapeDtypeStruct((t, n), BF16),
        scratch_shapes=[pltpu.VMEM((d, n), BF16)],
        compiler_params=_params("arbitrary"),
        name="mem_kv",
    )(mem, g, w)


def _merge_cross_kernel(x_ref, oa_ref, ob_ref, ga0_ref, ga1_ref, gb0_ref, gb1_ref, woa_ref, wob_ref, wout_ref,
                        gc_ref, wcq_ref, mkv_ref, wco_ref, o_ref, *, q_scale):
    ya = jnp.dot(oa_ref[...], woa_ref[...], preferred_element_type=F32)
    yb = jnp.dot(ob_ref[...], wob_ref[...], preferred_element_type=F32)
    ga = _sigmoid(jnp.concatenate([ga0_ref[...], ga1_ref[...]], axis=1).astype(F32))
    gb = _sigmoid(jnp.concatenate([gb0_ref[...], gb1_ref[...]], axis=1).astype(F32))
    merged = ga * ya + gb * yb
    x1 = x_ref[...] + jnp.dot(merged.astype(BF16), wout_ref[...], preferred_element_type=F32)

    hn = _rms(x1, gc_ref[...]).astype(BF16)
    qc = (jnp.dot(hn, wcq_ref[...], preferred_element_type=F32) * q_scale).astype(BF16)
    hd = XATTN_HEAD_DIM
    heads = []
    for h in range(XATTN_HEADS):
        k_h = mkv_ref[:, 2 * h * hd:(2 * h + 1) * hd]
        v_h = mkv_ref[:, (2 * h + 1) * hd:(2 * h + 2) * hd]
        s = lax.dot_general(qc[:, h * hd:(h + 1) * hd], k_h, _NT, preferred_element_type=F32)
        e, l = _softmax_parts(s)
        o_h = jnp.dot(e.astype(BF16), v_h, preferred_element_type=F32) * (1.0 / l)
        heads.append(o_h.astype(BF16))
    oc = jnp.concatenate(heads, axis=1)
    o_ref[...] = x1 + jnp.dot(oc, wco_ref[...], preferred_element_type=F32)


def _merge_cross(x, oa, ob, z, gate_block, woa, wob, wout, gc, wcq, mkv, wco, *, tm, batch):
    t, d = x.shape
    blocks_per_seq = t // batch // tm
    m = mkv.shape[0] // batch
    const = lambda a: pl.BlockSpec(a.shape, lambda i: (0, 0), pipeline_mode=pl.Buffered(1))
    rows = lambda width: pl.BlockSpec((tm, width), lambda i: (i, 0))
    return pl.pallas_call(
        functools.partial(_merge_cross_kernel, q_scale=LOG2_E * XATTN_HEAD_DIM ** -0.5),
        grid=(t // tm,),
        in_specs=[rows(d), rows(oa.shape[1]), rows(ob.shape[1])]
                 + [pl.BlockSpec((tm, d // 2), lambda i, k=k: (i, gate_block + k)) for k in range(4)]
                 + [const(woa), const(wob), const(wout), const(gc), const(wcq),
                  pl.BlockSpec((m, mkv.shape[1]), lambda i: (i // blocks_per_seq, 0)),
                  const(wco)],
        out_specs=rows(d),
        out_shape=jax.ShapeDtypeStruct((t, d), F32),
        compiler_params=_params("parallel"),
        name="merge_cross",
    )(x, oa, ob, z, z, z, z, woa, wob, wout, gc, wcq, mkv, wco)


def _conv_ffn_kernel(xp_ref, x_ref, xn_ref, g_ref, wga_ref, wgb_ref, wva_ref, wvb_ref, c_ref, wda_ref, wdb_ref,
                     gfin_ref, o_ref, h_ref, uga_ref, uva_ref, ugb_ref, uvb_ref, acc_ref,
                     *, tm, halo, blocks_per_seq, n_chunks, final_norm):
    i, f = pl.program_id(0), pl.program_id(1)
    last = pl.num_programs(1) - 1

    @pl.when(f == 0)
    def _():
        g = g_ref[...]
        pos = i % blocks_per_seq
        keep_prev = jnp.where(pos == 0, 0.0, 1.0)
        keep_next = jnp.where(pos == blocks_per_seq - 1, 0.0, 1.0)
        h_ref[0:halo, :] = (_rms(xp_ref[...], g) * keep_prev).astype(BF16)
        h_ref[halo:halo + tm, :] = _rms(x_ref[...], g).astype(BF16)
        h_ref[halo + tm:, :] = (_rms(xn_ref[...], g) * keep_next).astype(BF16)
        acc_ref[...] = jnp.zeros_like(acc_ref)

    def conv(u_ref, c):
        return (u_ref[halo - 1:halo - 1 + tm, :] * c[0:1, :] + u_ref[halo:halo + tm, :] * c[1:2, :]
                + u_ref[halo + 1:halo + 1 + tm, :] * c[2:3, :] + c[3:4, :])

    def chunk(idx, wg_ref, wv_ref, wd_ref, ug_ref, uv_ref):
        h = h_ref[...]
        ug_ref[...] = jnp.dot(h, wg_ref[...], preferred_element_type=F32)
        uv_ref[...] = jnp.dot(h, wv_ref[...], preferred_element_type=F32)
        yg = conv(ug_ref, c_ref[idx])
        yv = conv(uv_ref, c_ref[n_chunks + idx])
        act = (yg * _sigmoid(yg) * yv).astype(BF16)
        return jnp.dot(act, wd_ref[...], preferred_element_type=F32)

    @pl.when(f < last)
    def _():
        acc_ref[...] += (chunk(2 * f, wga_ref, wva_ref, wda_ref, uga_ref, uva_ref)
                         + chunk(2 * f + 1, wgb_ref, wvb_ref, wdb_ref, ugb_ref, uvb_ref))

    @pl.when(f == last)
    def _():
        y = x_ref[...] + acc_ref[...] + chunk(2 * f, wga_ref, wva_ref, wda_ref, uga_ref, uva_ref)
        o_ref[...] = _rms(y, gfin_ref[...]) if final_norm else y


def _conv_ffn(x, g, w_up, conv_w, conv_b, w_down, g_final, *, tm, tf, batch, final_norm):
    t, d = x.shape
    ffn = w_down.shape[0]
    n_chunks = ffn // tf
    assert n_chunks % 2 == 1
    steps = n_chunks // 2 + 1
    halo = BF16_SUBLANES
    blocks_per_seq = t // batch // tm
    hb = tm // halo
    last_halo_block = t // halo - 1
    kern = functools.partial(_conv_ffn_kernel, tm=tm, halo=halo, blocks_per_seq=blocks_per_seq,
                             n_chunks=n_chunks, final_norm=final_norm)
    conv_wb = jnp.concatenate([conv_w, conv_b], axis=0)
    conv_wb = conv_wb.reshape(conv_wb.shape[0], 2 * n_chunks, tf).transpose(1, 0, 2)
    ca = lambda f: 2 * f
    cb = lambda f: jnp.minimum(2 * f + 1, n_chunks - 1)
    return pl.pallas_call(
        kern,
        grid=(t // tm, steps),
        in_specs=[pl.BlockSpec((halo, d), lambda i, f: (jnp.maximum(i * hb - 1, 0), 0)),
                  pl.BlockSpec((tm, d), lambda i, f: (i, 0)),
                  pl.BlockSpec((halo, d), lambda i, f: (jnp.minimum((i + 1) * hb, last_halo_block), 0)),
                  pl.BlockSpec((1, d), lambda i, f: (0, 0)),
                  pl.BlockSpec((d, tf), lambda i, f: (0, ca(f))),
                  pl.BlockSpec((d, tf), lambda i, f: (0, cb(f))),
                  pl.BlockSpec((d, tf), lambda i, f: (0, n_chunks + ca(f))),
                  pl.BlockSpec((d, tf), lambda i, f: (0, n_chunks + cb(f))),
                  pl.BlockSpec(conv_wb.shape, lambda i, f: (0, 0, 0)),
                  pl.BlockSpec((tf, d), lambda i, f: (ca(f), 0)),
                  pl.BlockSpec((tf, d), lambda i, f: (cb(f), 0)),
                  pl.BlockSpec((1, d), lambda i, f: (0, 0))],
        out_specs=pl.BlockSpec((tm, d), lambda i, f: (i, 0)),
        out_shape=jax.ShapeDtypeStruct((t, d), F32),
        scratch_shapes=[pltpu.VMEM((tm + 2 * halo, d), BF16)]
                       + [pltpu.VMEM((tm + 2 * halo, tf), F32)] * 4
                       + [pltpu.VMEM((tm, d), F32)],
        compiler_params=_params("parallel", "arbitrary"),
        name="conv_ffn",
    )(x, x, x, g, w_up, w_up, w_up, w_up, conv_wb, w_down, w_down, g_final)


def _rope_tables(positions):
    dim = MLA_ROPE_DIM
    inv = ROPE_THETA ** (-jnp.arange(0, dim, 2, dtype=F32) / dim)
    inv_t = jnp.tile(inv, LANES // (dim // 2))
    sign_t = jnp.tile(jnp.repeat(jnp.array([-1.0, 1.0], F32), dim // 2), LANES // dim)
    ang = positions.astype(F32).reshape(-1, 1) * inv_t
    return jnp.cos(ang), jnp.sin(ang) * sign_t


def kernel(x, mem, positions, g_mix_norm, w_in, g_q_norm, w_uq, g_kv_norm, w_ukv, w_o_mla, lambda_q1, lambda_k1, lambda_q2, lambda_k2, g_diff_sub, w_o_diff, w_out, g_cross_norm, g_mem_norm, w_cross_q, w_cross_kv, w_cross_o, g_ffn_norm, w_up, conv_w, conv_b, w_down, g_final):
    batch, seq, d = x.shape
    depth = w_in.shape[0]
    t = batch * seq
    assert depth >= 1 and MLA_ROPE_DIM == DIFF_HEAD_DIM and MLA_NOPE_DIM == MLA_V_DIM
    q_rank, kv_rank = g_q_norm.shape[1], g_kv_norm.shape[1]
    dqk = DIFF_HEADS * 2 * DIFF_HEAD_DIM
    dv_w = DIFF_HEADS * DIFF_V_DIM
    c0 = q_rank + kv_rank + MLA_ROPE_DIM
    tn = COLS_IN_PROJ
    assert dqk == tn and dv_w == tn and (2 * d) % tn == 0

    cos_t, sin_t = _rope_tables(positions)
    xf = x.reshape(t, d)
    memf = mem.reshape(-1, d)
    row = lambda v: v.reshape(1, -1)
    w_in_t = jnp.swapaxes(w_in, 1, 2)

    for l in range(depth):
        lam_init = 0.8 - 0.6 * math.exp(-0.3 * l)
        wq = jnp.pad(w_uq[l].reshape(q_rank, MLA_HEADS, MLA_NOPE_DIM + MLA_ROPE_DIM),
                     ((0, 0), (0, 0), (0, MLA_QK_PAD - MLA_NOPE_DIM - MLA_ROPE_DIM)))
        wq = wq.reshape(q_rank, MLA_HEADS * MLA_QK_PAD).astype(BF16)
        wkv = w_ukv[l].astype(BF16)

        q, k, v, xn = _mla_proj(xf, row(g_mix_norm[l]), w_in_t, l, c0 + LANES - MLA_ROPE_DIM,
                                row(g_q_norm[l]), row(g_kv_norm[l]), wq, wkv, cos_t, sin_t, tm=ROWS_MLA_PROJ)
        z = _in_proj(xn, w_in_t, l, tm=ROWS_IN_PROJ, tn=tn, first_row=c0, n_blocks=3 + 2 * d // tn)
        oa, w_up_b, w_down_b = _mla_attn(q, k, v, [w_up, w_down], l, batch=batch, tq=Q_CHUNK)
        ob, w_oa_b, w_ob_b, w_out_b, w_cq_b, w_co_b = _diff_attn(
            row(lambda_q1[l]), row(lambda_k1[l]), row(lambda_q2[l]), row(lambda_k2[l]), row(g_diff_sub[l]),
            cos_t, sin_t, z, [w_o_mla, w_o_diff, w_out, w_cross_q, w_cross_o], l, batch=batch, tq=Q_CHUNK,
            q_col=0, k_col=DIFF_HEADS, v_col=2 * DIFF_HEADS, lam_init=lam_init)
        mkv = _mem_kv(memf, row(g_mem_norm[l]), w_cross_kv, l, tm=memf.shape[0] // batch)
        xf = _merge_cross(xf, oa, ob, z, 3 * tn // (d // 2), w_oa_b, w_ob_b, w_out_b, row(g_cross_norm[l]), w_cq_b, mkv,
                          w_co_b, tm=ROWS_MERGE, batch=batch)
        xf = _conv_ffn(xf, row(g_ffn_norm[l]), w_up_b, conv_w[l], row(conv_b[l]),
                       w_down_b, row(g_final), tm=ROWS_FFN, tf=COLS_FFN, batch=batch,
                       final_norm=(l == depth - 1))
    return xf.reshape(batch, seq, d)
```

```python
import functools
import math

import jax
import jax.numpy as jnp
from jax import lax
from jax.experimental import pallas as pl
from jax.experimental.pallas import tpu as pltpu

F32 = jnp.float32
BF16 = jnp.bfloat16

EPS = 1e-6
LOG2_E = math.log2(math.e)
ROPE_THETA = 10000.0
MLA_HEADS = 8
MLA_NOPE_DIM = 128
MLA_ROPE_DIM = 64
MLA_V_DIM = 128
MLA_QK_PAD = 256
DIFF_HEADS = 8
DIFF_HEAD_DIM = 64
DIFF_V_DIM = 2 * DIFF_HEAD_DIM
XATTN_HEADS = 4
XATTN_HEAD_DIM = 128
LANES = 128
BF16_SUBLANES = 16
MXU_COLS = 256
VMEM_LIMIT_BYTES = 56 * 1024 * 1024

ROWS_MLA_PROJ = 512
ROWS_IN_PROJ = 2048
COLS_IN_PROJ = 1024
Q_CHUNK = 512
ROWS_MERGE = 512
ROWS_FFN = 512
COLS_FFN = 512

_NT = (((1,), (1,)), ((), ()))


def _params(*semantics):
    return pltpu.CompilerParams(dimension_semantics=semantics, vmem_limit_bytes=VMEM_LIMIT_BYTES)


def _rms(x, g):
    ms = jnp.mean(x * x, axis=-1, keepdims=True)
    return x * lax.rsqrt(ms + EPS) * g


def _sigmoid(x):
    return 1.0 / (1.0 + jnp.exp(-x))


def _rope128(x, cos_t, sin_t):
    lane = lax.broadcasted_iota(jnp.int32, x.shape, 1)
    first_half = (lane & (MLA_ROPE_DIM // 2)) == 0
    half = MLA_ROPE_DIM // 2
    partner = jnp.where(first_half, pltpu.roll(x, LANES - half, 1), pltpu.roll(x, half, 1))
    return x * cos_t + partner * sin_t


def _softmax_parts(s):
    m = jnp.max(s, axis=-1, keepdims=True)
    e = jnp.exp2(s - m)
    return e, jnp.sum(e, axis=-1, keepdims=True)


def _mla_proj_kernel(x_ref, g_ref, w1t_ref, gq_ref, gkv_ref, wq_ref, wkv_ref, cos_ref, sin_ref,
                     q_ref, k_ref, v_ref, xn_ref, w1_ref, *, q_rank, kv_rank, q_scale):
    @pl.when(pl.program_id(0) == 0)
    def _():
        w1_ref[...] = w1t_ref[...].astype(BF16)

    xn = _rms(x_ref[...], g_ref[...]).astype(BF16)
    xn_ref[...] = xn
    z = lax.dot_general(xn, w1_ref[...], _NT, preferred_element_type=F32)
    cqn = _rms(z[:, :q_rank], gq_ref[...]).astype(BF16)
    ckvn = _rms(z[:, q_rank:q_rank + kv_rank], gkv_ref[...]).astype(BF16)
    cos_t, sin_t = cos_ref[...], sin_ref[...]
    kpe = _rope128(z[:, q_rank + kv_rank:], cos_t, sin_t).astype(BF16)
    q = jnp.dot(cqn, wq_ref[...], preferred_element_type=F32) * q_scale
    kv = jnp.dot(ckvn, wkv_ref[...], preferred_element_type=F32)
    kv_w = MLA_NOPE_DIM + MLA_V_DIM
    for h in range(MLA_HEADS):
        lo, mid, hi = h * MLA_QK_PAD, h * MLA_QK_PAD + MLA_NOPE_DIM, (h + 1) * MLA_QK_PAD
        q_ref[:, lo:mid] = q[:, lo:mid].astype(BF16)
        q_ref[:, mid:hi] = _rope128(q[:, mid:hi], cos_t, sin_t).astype(BF16)
        k_ref[:, lo:mid] = kv[:, kv_w * h:kv_w * h + MLA_NOPE_DIM].astype(BF16)
        k_ref[:, mid:hi] = kpe
        v_ref[:, h * MLA_V_DIM:(h + 1) * MLA_V_DIM] = kv[:, kv_w * h + MLA_NOPE_DIM:kv_w * (h + 1)].astype(BF16)


def _mla_proj(x, g, w_t, layer, n1, gq, gkv, wq, wkv, cos_t, sin_t, *, tm):
    t, d = x.shape
    q_rank, kv_rank = gq.shape[1], gkv.shape[1]
    full = lambda a: pl.BlockSpec(a.shape, lambda i: (0, 0))
    rows = lambda width: pl.BlockSpec((tm, width), lambda i: (i, 0))
    qk_w, v_w = MLA_HEADS * MLA_QK_PAD, MLA_HEADS * MLA_V_DIM
    kern = functools.partial(_mla_proj_kernel, q_rank=q_rank, kv_rank=kv_rank,
                             q_scale=LOG2_E * (MLA_NOPE_DIM + MLA_ROPE_DIM) ** -0.5)
    return pl.pallas_call(
        kern,
        grid=(t // tm,),
        in_specs=[rows(d), full(g),
                  pl.BlockSpec((None, n1, d), lambda i: (layer, 0, 0), pipeline_mode=pl.Buffered(1)),
                  full(gq), full(gkv), full(wq), full(wkv), rows(LANES), rows(LANES)],
        out_specs=[rows(qk_w), rows(qk_w), rows(v_w), rows(d)],
        out_shape=[jax.ShapeDtypeStruct((t, qk_w), BF16), jax.ShapeDtypeStruct((t, qk_w), BF16),
                   jax.ShapeDtypeStruct((t, v_w), BF16), jax.ShapeDtypeStruct((t, d), BF16)],
        scratch_shapes=[pltpu.VMEM((n1, d), BF16)],
        compiler_params=_params("arbitrary"),
        name="mla_proj",
    )(x, g, w_t, gq, gkv, wq, wkv, cos_t, sin_t)


def _in_proj_kernel(xn_ref, wt_ref, o_ref, wbf_ref):
    @pl.when(pl.program_id(1) == 0)
    def _():
        wbf_ref[...] = wt_ref[0].astype(BF16)

    for c in range(0, o_ref.shape[1], MXU_COLS):
        z = lax.dot_general(xn_ref[...], wbf_ref[c:c + MXU_COLS, :], _NT, preferred_element_type=F32)
        o_ref[:, c:c + MXU_COLS] = z.astype(BF16)


def _in_proj(xn, w_t, layer, *, tm, tn, first_row, n_blocks):
    t, d = xn.shape
    w_spec = pl.BlockSpec((pl.Element(1), pl.Element(tn), pl.Element(d)),
                          lambda j, i: (layer, pl.multiple_of(first_row + j * tn, BF16_SUBLANES), 0))
    return pl.pallas_call(
        _in_proj_kernel,
        grid=(n_blocks, t // tm),
        in_specs=[pl.BlockSpec((tm, d), lambda j, i: (i, 0)), w_spec],
        out_specs=pl.BlockSpec((tm, tn), lambda j, i: (i, j)),
        out_shape=jax.ShapeDtypeStruct((t, n_blocks * tn), BF16),
        scratch_shapes=[pltpu.VMEM((tn, d), BF16)],
        compiler_params=_params("arbitrary", "arbitrary"),
        name="in_proj",
    )(xn, w_t)


def _cast_specs(weights, layer, steps, step_index):
    ins, outs, shapes = [], [], []
    for w in weights:
        _, r, c = w.shape
        rb = r // steps
        assert rb * steps == r and rb % BF16_SUBLANES == 0
        ins.append(pl.BlockSpec((None, rb, c), lambda *g: (layer, step_index(*g), 0)))
        outs.append(pl.BlockSpec((rb, c), lambda *g: (step_index(*g), 0)))
        shapes.append(jax.ShapeDtypeStruct((r, c), BF16))
    return ins, outs, shapes


def _run_casts(src_refs, dst_refs):
    for src, dst in zip(src_refs, dst_refs):
        dst[...] = src[...].astype(BF16)


def _rows(c, tq):
    if isinstance(c, int):
        return slice(c * tq, (c + 1) * tq)
    return pl.ds(pl.multiple_of(c * tq, tq), tq)


def _pipelined_chunks(n, qk, sm, pv):
    def step(c, parity):
        qk(c, parity)
        sm(1 - parity)
        pv(c - 2, parity)

    if n == 1:
        qk(0, 0)
        sm(0)
        pv(0, 0)
        return

    qk(0, 0)
    qk(1, 1)
    sm(0)

    def body(i, carry):
        step(2 * i, 0)
        step(2 * i + 1, 1)
        return carry

    lax.fori_loop(1, n // 2, body, 0)
    sm(1)
    pv(n - 2, 0)
    pv(n - 1, 1)


def _fill_v_ones(va_ref, v_ref):
    w = v_ref.shape[1]
    va_ref[:, :w] = v_ref[...]
    va_ref[:, w:] = jnp.ones((va_ref.shape[0], va_ref.shape[1] - w), BF16)


def _exp_scores(s):
    return jnp.exp2(s - jnp.max(s, axis=-1, keepdims=True)).astype(BF16)


def _mla_attn_kernel(q_ref, k_ref, v_ref, *refs, tq, n_cast):
    o_ref = refs[n_cast]
    s0, s1, p0, p1, va_ref = refs[2 * n_cast + 1:]
    _run_casts(refs[:n_cast], refs[n_cast + 1:2 * n_cast + 1])
    sb, pb = (s0, s1), (p0, p1)
    dv = v_ref.shape[1]
    _fill_v_ones(va_ref, v_ref)

    def qk(c, slot):
        sb[slot][...] = lax.dot_general(q_ref[_rows(c, tq), :], k_ref[...], _NT, preferred_element_type=F32)

    def sm(slot):
        pb[slot][...] = _exp_scores(sb[slot][...])

    def pv(c, slot):
        o = jnp.dot(pb[slot][...], va_ref[...], preferred_element_type=F32)
        o_ref[_rows(c, tq), :] = (o[:, :dv] * (1.0 / o[:, dv:])).astype(BF16)

    _pipelined_chunks(q_ref.shape[0] // tq, qk, sm, pv)


def _mla_attn(q, k, v, cast_weights, layer, *, batch, tq):
    t = q.shape[0]
    s = t // batch
    c_in, c_out, c_shapes = _cast_specs(cast_weights, layer, batch * MLA_HEADS, lambda b, h: b * MLA_HEADS + h)
    return pl.pallas_call(
        functools.partial(_mla_attn_kernel, tq=tq, n_cast=len(cast_weights)),
        grid=(batch, MLA_HEADS),
        in_specs=[pl.BlockSpec((s, MLA_QK_PAD), lambda b, h: (b, h)),
                  pl.BlockSpec((s, MLA_QK_PAD), lambda b, h: (b, h)),
                  pl.BlockSpec((s, MLA_V_DIM), lambda b, h: (b, h))] + c_in,
        out_specs=[pl.BlockSpec((s, MLA_V_DIM), lambda b, h: (b, h))] + c_out,
        out_shape=[jax.ShapeDtypeStruct((t, MLA_HEADS * MLA_V_DIM), BF16)] + c_shapes,
        scratch_shapes=[pltpu.VMEM((tq, s), F32), pltpu.VMEM((tq, s), F32),
                        pltpu.VMEM((tq, s), BF16), pltpu.VMEM((tq, s), BF16),
                        pltpu.VMEM((s, 2 * MLA_V_DIM), BF16)],
        compiler_params=_params("parallel", "parallel"),
        name="mla_attn",
    )(q, k, v, *cast_weights)


def _diff_attn_kernel(lq1_ref, lk1_ref, lq2_ref, lk2_ref, gsub_ref, cos_ref, sin_ref, q_ref, k_ref, v_ref,
                      *refs, tq, lam_init, q_scale, n_cast):
    o_ref = refs[n_cast]
    s0, s1, p0, p1, va_ref, kr_ref = refs[2 * n_cast + 1:]
    _run_casts(refs[:n_cast], refs[n_cast + 1:2 * n_cast + 1])
    lam = (jnp.exp(jnp.sum(lq1_ref[...] * lk1_ref[...], axis=-1, keepdims=True))
           - jnp.exp(jnp.sum(lq2_ref[...] * lk2_ref[...], axis=-1, keepdims=True)) + lam_init)
    gain = gsub_ref[...] * (1.0 - lam_init)
    lane = lax.broadcasted_iota(jnp.int32, (tq, 2 * DIFF_HEAD_DIM), 1)
    map0 = lane < DIFF_HEAD_DIM
    sb, pb = (s0, s1), (p0, p1)
    dv = v_ref.shape[1]
    _fill_v_ones(va_ref, v_ref)
    kr_ref[...] = _rope128(k_ref[...].astype(F32), cos_ref[...], sin_ref[...]).astype(BF16)

    def qk(c, slot):
        rows = _rows(c, tq)
        q = _rope128(q_ref[rows, :].astype(F32), cos_ref[rows, :], sin_ref[rows, :]) * q_scale
        q = q.astype(BF16)
        zero = jnp.zeros_like(q)
        q01 = jnp.concatenate([jnp.where(map0, q, zero), jnp.where(map0, zero, q)], axis=0)
        sb[slot][...] = lax.dot_general(q01, kr_ref[...], _NT, preferred_element_type=F32)

    def sm(slot):
        pb[slot][...] = _exp_scores(sb[slot][...])

    def pv(c, slot):
        o = jnp.dot(pb[slot][...], va_ref[...], preferred_element_type=F32)
        o = o[:, :dv] * (1.0 / o[:, dv:])
        o_ref[_rows(c, tq), :] = _rms(o[:tq] - lam * o[tq:], gain).astype(BF16)

    _pipelined_chunks(q_ref.shape[0] // tq, qk, sm, pv)


def _diff_attn(lq1, lk1, lq2, lk2, gsub, cos_t, sin_t, z, cast_weights, layer,
               *, batch, tq, q_col, k_col, v_col, lam_init):
    t = z.shape[0]
    s = t // batch
    hw = 2 * DIFF_HEAD_DIM
    small = lambda a: pl.BlockSpec(a.shape, lambda b, h: (0, 0))
    head = lambda col: pl.BlockSpec((s, hw), lambda b, h: (b, col + h))
    c_in, c_out, c_shapes = _cast_specs(cast_weights, layer, batch * DIFF_HEADS, lambda b, h: b * DIFF_HEADS + h)
    return pl.pallas_call(
        functools.partial(_diff_attn_kernel, tq=tq, lam_init=lam_init, n_cast=len(cast_weights),
                          q_scale=LOG2_E * DIFF_HEAD_DIM ** -0.5),
        grid=(batch, DIFF_HEADS),
        in_specs=[small(lq1), small(lk1), small(lq2), small(lk2), small(gsub),
                  pl.BlockSpec((s, LANES), lambda b, h: (b, 0)), pl.BlockSpec((s, LANES), lambda b, h: (b, 0)),
                  head(q_col), head(k_col), head(v_col)] + c_in,
        out_specs=[pl.BlockSpec((s, DIFF_V_DIM), lambda b, h: (b, h))] + c_out,
        out_shape=[jax.ShapeDtypeStruct((t, DIFF_HEADS * DIFF_V_DIM), BF16)] + c_shapes,
        scratch_shapes=[pltpu.VMEM((2 * tq, s), F32), pltpu.VMEM((2 * tq, s), F32),
                        pltpu.VMEM((2 * tq, s), BF16), pltpu.VMEM((2 * tq, s), BF16),
                        pltpu.VMEM((s, 2 * DIFF_V_DIM), BF16), pltpu.VMEM((s, hw), BF16)],
        compiler_params=_params("parallel", "parallel"),
        name="diff_attn",
    )(lq1, lk1, lq2, lk2, gsub, cos_t, sin_t, z, z, z, *cast_weights)


def _mem_kv_kernel(m_ref, g_ref, w_ref, o_ref, wbf_ref):
    @pl.when(pl.program_id(0) == 0)
    def _():
        wbf_ref[...] = w_ref[...].astype(BF16)

    mn = _rms(m_ref[...], g_ref[...]).astype(BF16)
    o_ref[...] = jnp.dot(mn, wbf_ref[...], preferred_element_type=F32).astype(BF16)


def _mem_kv(mem, g, w, layer, *, tm):
    t, d = mem.shape
    n = w.shape[2]
    return pl.pallas_call(
        _mem_kv_kernel,
        grid=(t // tm,),
        in_specs=[pl.BlockSpec((tm, d), lambda i: (i, 0)),
                  pl.BlockSpec((1, d), lambda i: (0, 0)),
                  pl.BlockSpec((None, d, n), lambda i: (layer, 0, 0), pipeline_mode=pl.Buffered(1))],
        out_specs=pl.BlockSpec((tm, n), lambda i: (i, 0)),
        out_shape=jax.ShapeDtypeStruct((t, n), BF16),
        scratch_shapes=[pltpu.VMEM((d, n), BF16)],
        compiler_params=_params("arbitrary"),
        name="mem_kv",
    )(mem, g, w)


def _merge_cross_kernel(x_ref, oa_ref, ob_ref, ga0_ref, ga1_ref, gb0_ref, gb1_ref, woa_ref, wob_ref, wout_ref,
                        gc_ref, wcq_ref, mkv_ref, wco_ref, o_ref, *, q_scale):
    ya = jnp.dot(oa_ref[...], woa_ref[...], preferred_element_type=F32)
    yb = jnp.dot(ob_ref[...], wob_ref[...], preferred_element_type=F32)
    ga = _sigmoid(jnp.concatenate([ga0_ref[...], ga1_ref[...]], axis=1).astype(F32))
    gb = _sigmoid(jnp.concatenate([gb0_ref[...], gb1_ref[...]], axis=1).astype(F32))
    merged = ga * ya + gb * yb
    x1 = x_ref[...] + jnp.dot(merged.astype(BF16), wout_ref[...], preferred_element_type=F32)

    hn = _rms(x1, gc_ref[...]).astype(BF16)
    qc = (jnp.dot(hn, wcq_ref[...], preferred_element_type=F32) * q_scale).astype(BF16)
    hd = XATTN_HEAD_DIM
    heads = []
    for h in range(XATTN_HEADS):
        k_h = mkv_ref[:, 2 * h * hd:(2 * h + 1) * hd]
        v_h = mkv_ref[:, (2 * h + 1) * hd:(2 * h + 2) * hd]
        s = lax.dot_general(qc[:, h * hd:(h + 1) * hd], k_h, _NT, preferred_element_type=F32)
        e, l = _softmax_parts(s)
        o_h = jnp.dot(e.astype(BF16), v_h, preferred_element_type=F32) * (1.0 / l)
        heads.append(o_h.astype(BF16))
    oc = jnp.concatenate(heads, axis=1)
    o_ref[...] = x1 + jnp.dot(oc, wco_ref[...], preferred_element_type=F32)


def _merge_cross(x, oa, ob, z, gate_block, woa, wob, wout, gc, wcq, mkv, wco, *, tm, batch):
    t, d = x.shape
    blocks_per_seq = t // batch // tm
    m = mkv.shape[0] // batch
    const = lambda a: pl.BlockSpec(a.shape, lambda i: (0, 0), pipeline_mode=pl.Buffered(1))
    rows = lambda width: pl.BlockSpec((tm, width), lambda i: (i, 0))
    return pl.pallas_call(
        functools.partial(_merge_cross_kernel, q_scale=LOG2_E * XATTN_HEAD_DIM ** -0.5),
        grid=(t // tm,),
        in_specs=[rows(d), rows(oa.shape[1]), rows(ob.shape[1])]
                 + [pl.BlockSpec((tm, d // 2), lambda i, k=k: (i, gate_block + k)) for k in range(4)]
                 + [const(woa), const(wob), const(wout), const(gc), const(wcq),
                  pl.BlockSpec((m, mkv.shape[1]), lambda i: (i // blocks_per_seq, 0)),
                  const(wco)],
        out_specs=rows(d),
        out_shape=jax.ShapeDtypeStruct((t, d), F32),
        compiler_params=_params("parallel"),
        name="merge_cross",
    )(x, oa, ob, z, z, z, z, woa, wob, wout, gc, wcq, mkv, wco)


def _conv_ffn_kernel(xp_ref, x_ref, xn_ref, g_ref, wga_ref, wgb_ref, wva_ref, wvb_ref, c_ref, wda_ref, wdb_ref,
                     gfin_ref, o_ref, h_ref, uga_ref, uva_ref, ugb_ref, uvb_ref, acc_ref,
                     *, tm, halo, blocks_per_seq, n_chunks, final_norm):
    i, f = pl.program_id(0), pl.program_id(1)
    last = pl.num_programs(1) - 1

    @pl.when(f == 0)
    def _():
        g = g_ref[...]
        pos = i % blocks_per_seq
        keep_prev = jnp.where(pos == 0, 0.0, 1.0)
        keep_next = jnp.where(pos == blocks_per_seq - 1, 0.0, 1.0)
        h_ref[0:halo, :] = (_rms(xp_ref[...], g) * keep_prev).astype(BF16)
        h_ref[halo:halo + tm, :] = _rms(x_ref[...], g).astype(BF16)
        h_ref[halo + tm:, :] = (_rms(xn_ref[...], g) * keep_next).astype(BF16)
        acc_ref[...] = jnp.zeros_like(acc_ref)

    def conv(u_ref, c):
        return (u_ref[halo - 1:halo - 1 + tm, :] * c[0:1, :] + u_ref[halo:halo + tm, :] * c[1:2, :]
                + u_ref[halo + 1:halo + 1 + tm, :] * c[2:3, :] + c[3:4, :])

    def chunk(idx, wg_ref, wv_ref, wd_ref, ug_ref, uv_ref):
        h = h_ref[...]
        ug_ref[...] = jnp.dot(h, wg_ref[...], preferred_element_type=F32)
        uv_ref[...] = jnp.dot(h, wv_ref[...], preferred_element_type=F32)
        yg = conv(ug_ref, c_ref[idx])
        yv = conv(uv_ref, c_ref[n_chunks + idx])
        act = (yg * _sigmoid(yg) * yv).astype(BF16)
        return jnp.dot(act, wd_ref[...], preferred_element_type=F32)

    @pl.when(f < last)
    def _():
        acc_ref[...] += (chunk(2 * f, wga_ref, wva_ref, wda_ref, uga_ref, uva_ref)
                         + chunk(2 * f + 1, wgb_ref, wvb_ref, wdb_ref, ugb_ref, uvb_ref))

    @pl.when(f == last)
    def _():
        y = x_ref[...] + acc_ref[...] + chunk(2 * f, wga_ref, wva_ref, wda_ref, uga_ref, uva_ref)
        o_ref[...] = _rms(y, gfin_ref[...]) if final_norm else y


def _conv_ffn(x, g, w_up, conv_w, conv_b, w_down, g_final, *, tm, tf, batch, final_norm):
    t, d = x.shape
    ffn = w_down.shape[0]
    n_chunks = ffn // tf
    assert n_chunks % 2 == 1
    steps = n_chunks // 2 + 1
    halo = BF16_SUBLANES
    blocks_per_seq = t // batch // tm
    hb = tm // halo
    last_halo_block = t // halo - 1
    kern = functools.partial(_conv_ffn_kernel, tm=tm, halo=halo, blocks_per_seq=blocks_per_seq,
                             n_chunks=n_chunks, final_norm=final_norm)
    conv_wb = jnp.concatenate([conv_w, conv_b], axis=0)
    conv_wb = conv_wb.reshape(conv_wb.shape[0], 2 * n_chunks, tf).transpose(1, 0, 2)
    ca = lambda f: 2 * f
    cb = lambda f: jnp.minimum(2 * f + 1, n_chunks - 1)
    return pl.pallas_call(
        kern,
        grid=(t // tm, steps),
        in_specs=[pl.BlockSpec((halo, d), lambda i, f: (jnp.maximum(i * hb - 1, 0), 0)),
                  pl.BlockSpec((tm, d), lambda i, f: (i, 0)),
                  pl.BlockSpec((halo, d), lambda i, f: (jnp.minimum((i + 1) * hb, last_halo_block), 0)),
                  pl.BlockSpec((1, d), lambda i, f: (0, 0)),
                  pl.BlockSpec((d, tf), lambda i, f: (0, ca(f))),
                  pl.BlockSpec((d, tf), lambda i, f: (0, cb(f))),
                  pl.BlockSpec((d, tf), lambda i, f: (0, n_chunks + ca(f))),
                  pl.BlockSpec((d, tf), lambda i, f: (0, n_chunks + cb(f))),
                  pl.BlockSpec(conv_wb.shape, lambda i, f: (0, 0, 0)),
                  pl.BlockSpec((tf, d), lambda i, f: (ca(f), 0)),
                  pl.BlockSpec((tf, d), lambda i, f: (cb(f), 0)),
                  pl.BlockSpec((1, d), lambda i, f: (0, 0))],
        out_specs=pl.BlockSpec((tm, d), lambda i, f: (i, 0)),
        out_shape=jax.ShapeDtypeStruct((t, d), F32),
        scratch_shapes=[pltpu.VMEM((tm + 2 * halo, d), BF16)]
                       + [pltpu.VMEM((tm + 2 * halo, tf), F32)] * 4
                       + [pltpu.VMEM((tm, d), F32)],
        compiler_params=_params("parallel", "arbitrary"),
        name="conv_ffn",
    )(x, x, x, g, w_up, w_up, w_up, w_up, conv_wb, w_down, w_down, g_final)


def _rope_tables(positions):
    dim = MLA_ROPE_DIM
    inv = ROPE_THETA ** (-jnp.arange(0, dim, 2, dtype=F32) / dim)
    inv_t = jnp.tile(inv, LANES // (dim // 2))
    sign_t = jnp.tile(jnp.repeat(jnp.array([-1.0, 1.0], F32), dim // 2), LANES // dim)
    ang = positions.astype(F32).reshape(-1, 1) * inv_t
    return jnp.cos(ang), jnp.sin(ang) * sign_t


def kernel(x, mem, positions, g_mix_norm, w_in, g_q_norm, w_uq, g_kv_norm, w_ukv, w_o_mla, lambda_q1, lambda_k1, lambda_q2, lambda_k2, g_diff_sub, w_o_diff, w_out, g_cross_norm, g_mem_norm, w_cross_q, w_cross_kv, w_cross_o, g_ffn_norm, w_up, conv_w, conv_b, w_down, g_final):
    batch, seq, d = x.shape
    depth = w_in.shape[0]
    t = batch * seq
    assert depth >= 1 and MLA_ROPE_DIM == DIFF_HEAD_DIM and MLA_NOPE_DIM == MLA_V_DIM
    q_rank, kv_rank = g_q_norm.shape[1], g_kv_norm.shape[1]
    dqk = DIFF_HEADS * 2 * DIFF_HEAD_DIM
    dv_w = DIFF_HEADS * DIFF_V_DIM
    c0 = q_rank + kv_rank + MLA_ROPE_DIM
    tn = COLS_IN_PROJ
    assert dqk == tn and dv_w == tn and (2 * d) % tn == 0

    cos_t, sin_t = _rope_tables(positions)
    xf = x.reshape(t, d)
    memf = mem.reshape(-1, d)
    row = lambda v: v.reshape(1, -1)
    w_in_t = jnp.swapaxes(w_in, 1, 2)

    for l in range(depth):
        lam_init = 0.8 - 0.6 * math.exp(-0.3 * l)
        wq = jnp.pad(w_uq[l].reshape(q_rank, MLA_HEADS, MLA_NOPE_DIM + MLA_ROPE_DIM),
                     ((0, 0), (0, 0), (0, MLA_QK_PAD - MLA_NOPE_DIM - MLA_ROPE_DIM)))
        wq = wq.reshape(q_rank, MLA_HEADS * MLA_QK_PAD).astype(BF16)
        wkv = w_ukv[l].astype(BF16)

        q, k, v, xn = _mla_proj(xf, row(g_mix_norm[l]), w_in_t, l, c0 + LANES - MLA_ROPE_DIM,
                                row(g_q_norm[l]), row(g_kv_norm[l]), wq, wkv, cos_t, sin_t, tm=ROWS_MLA_PROJ)
        z = _in_proj(xn, w_in_t, l, tm=ROWS_IN_PROJ, tn=tn, first_row=c0, n_blocks=3 + 2 * d // tn)
        oa, w_up_b, w_down_b = _mla_attn(q, k, v, [w_up, w_down], l, batch=batch, tq=seq)
        ob, w_oa_b, w_ob_b, w_out_b, w_cq_b, w_co_b = _diff_attn(
            row(lambda_q1[l]), row(lambda_k1[l]), row(lambda_q2[l]), row(lambda_k2[l]), row(g_diff_sub[l]),
            cos_t, sin_t, z, [w_o_mla, w_o_diff, w_out, w_cross_q, w_cross_o], l, batch=batch, tq=Q_CHUNK,
            q_col=0, k_col=DIFF_HEADS, v_col=2 * DIFF_HEADS, lam_init=lam_init)
        mkv = _mem_kv(memf, row(g_mem_norm[l]), w_cross_kv, l, tm=memf.shape[0] // batch)
        xf = _merge_cross(xf, oa, ob, z, 3 * tn // (d // 2), w_oa_b, w_ob_b, w_out_b, row(g_cross_norm[l]), w_cq_b, mkv,
                          w_co_b, tm=ROWS_MERGE, batch=batch)
        xf = _conv_ffn(xf, row(g_ffn_norm[l]), w_up_b, conv_w[l], row(conv_b[l]),
                       w_down_b, row(g_final), tm=ROWS_FFN, tf=COLS_FFN, batch=batch,
                       final_norm=(l == depth - 1))
    return xf.reshape(batch, seq, d)
```

```python
import functools
import math

import jax
import jax.numpy as jnp
from jax import lax
from jax.experimental import pallas as pl
from jax.experimental.pallas import tpu as pltpu

F32 = jnp.float32
BF16 = jnp.bfloat16

EPS = 1e-6
LOG2_E = math.log2(math.e)
ROPE_THETA = 10000.0
MLA_HEADS = 8
MLA_NOPE_DIM = 128
MLA_ROPE_DIM = 64
MLA_V_DIM = 128
MLA_QK_PAD = 256
DIFF_HEADS = 8
DIFF_HEAD_DIM = 64
DIFF_V_DIM = 2 * DIFF_HEAD_DIM
XATTN_HEADS = 4
XATTN_HEAD_DIM = 128
LANES = 128
BF16_SUBLANES = 16
MXU_COLS = 256
VMEM_LIMIT_BYTES = 56 * 1024 * 1024

ROWS_MLA_PROJ = 512
ROWS_IN_PROJ = 2048
COLS_IN_PROJ = 1024
Q_CHUNK = 512
ROWS_MERGE = 512
ROWS_FFN = 512
COLS_FFN = 512

_NT = (((1,), (1,)), ((), ()))


def _params(*semantics):
    return pltpu.CompilerParams(dimension_semantics=semantics, vmem_limit_bytes=VMEM_LIMIT_BYTES)


def _rms(x, g):
    ms = jnp.mean(x * x, axis=-1, keepdims=True)
    return x * lax.rsqrt(ms + EPS) * g


def _sigmoid(x):
    return 1.0 / (1.0 + jnp.exp(-x))


def _rope128(x, cos_t, sin_t):
    lane = lax.broadcasted_iota(jnp.int32, x.shape, 1)
    first_half = (lane & (MLA_ROPE_DIM // 2)) == 0
    half = MLA_ROPE_DIM // 2
    partner = jnp.where(first_half, pltpu.roll(x, LANES - half, 1), pltpu.roll(x, half, 1))
    return x * cos_t + partner * sin_t


def _softmax_parts(s):
    m = jnp.max(s, axis=-1, keepdims=True)
    e = jnp.exp2(s - m)
    return e, jnp.sum(e, axis=-1, keepdims=True)


def _mla_proj_kernel(x_ref, g_ref, w1t_ref, gq_ref, gkv_ref, wq_ref, wkv_ref, cos_ref, sin_ref,
                     q_ref, k_ref, v_ref, xn_ref, w1_ref, *, q_rank, kv_rank, q_scale):
    @pl.when(pl.program_id(0) == 0)
    def _():
        w1_ref[...] = w1t_ref[...].astype(BF16)

    xn = _rms(x_ref[...], g_ref[...]).astype(BF16)
    xn_ref[...] = xn
    z = lax.dot_general(xn, w1_ref[...], _NT, preferred_element_type=F32)
    cqn = _rms(z[:, :q_rank], gq_ref[...]).astype(BF16)
    ckvn = _rms(z[:, q_rank:q_rank + kv_rank], gkv_ref[...]).astype(BF16)
    cos_t, sin_t = cos_ref[...], sin_ref[...]
    kpe = _rope128(z[:, q_rank + kv_rank:], cos_t, sin_t).astype(BF16)
    q = jnp.dot(cqn, wq_ref[...], preferred_element_type=F32) * q_scale
    kv = jnp.dot(ckvn, wkv_ref[...], preferred_element_type=F32)
    kv_w = MLA_NOPE_DIM + MLA_V_DIM
    for h in range(MLA_HEADS):
        lo, mid, hi = h * MLA_QK_PAD, h * MLA_QK_PAD + MLA_NOPE_DIM, (h + 1) * MLA_QK_PAD
        q_ref[:, lo:mid] = q[:, lo:mid].astype(BF16)
        q_ref[:, mid:hi] = _rope128(q[:, mid:hi], cos_t, sin_t).astype(BF16)
        k_ref[:, lo:mid] = kv[:, kv_w * h:kv_w * h + MLA_NOPE_DIM].astype(BF16)
        k_ref[:, mid:hi] = kpe
        v_ref[:, h * MLA_V_DIM:(h + 1) * MLA_V_DIM] = kv[:, kv_w * h + MLA_NOPE_DIM:kv_w * (h + 1)].astype(BF16)


def _mla_proj(x, g, w_t, layer, n1, gq, gkv, wq, wkv, cos_t, sin_t, *, tm):
    t, d = x.shape
    q_rank, kv_rank = gq.shape[1], gkv.shape[1]
    full = lambda a: pl.BlockSpec(a.shape, lambda i: (0, 0))
    rows = lambda width: pl.BlockSpec((tm, width), lambda i: (i, 0))
    qk_w, v_w = MLA_HEADS * MLA_QK_PAD, MLA_HEADS * MLA_V_DIM
    kern = functools.partial(_mla_proj_kernel, q_rank=q_rank, kv_rank=kv_rank,
                             q_scale=LOG2_E * (MLA_NOPE_DIM + MLA_ROPE_DIM) ** -0.5)
    return pl.pallas_call(
        kern,
        grid=(t // tm,),
        in_specs=[rows(d), full(g),
                  pl.BlockSpec((None, n1, d), lambda i: (layer, 0, 0), pipeline_mode=pl.Buffered(1)),
                  full(gq), full(gkv), full(wq), full(wkv), rows(LANES), rows(LANES)],
        out_specs=[rows(qk_w), rows(qk_w), rows(v_w), rows(d)],
        out_shape=[jax.ShapeDtypeStruct((t, qk_w), BF16), jax.ShapeDtypeStruct((t, qk_w), BF16),
                   jax.ShapeDtypeStruct((t, v_w), BF16), jax.ShapeDtypeStruct((t, d), BF16)],
        scratch_shapes=[pltpu.VMEM((n1, d), BF16)],
        compiler_params=_params("arbitrary"),
        name="mla_proj",
    )(x, g, w_t, gq, gkv, wq, wkv, cos_t, sin_t)


def _in_proj_kernel(xn_ref, wt_ref, o_ref, wbf_ref):
    @pl.when(pl.program_id(1) == 0)
    def _():
        wbf_ref[...] = wt_ref[0].astype(BF16)

    for c in range(0, o_ref.shape[1], MXU_COLS):
        z = lax.dot_general(xn_ref[...], wbf_ref[c:c + MXU_COLS, :], _NT, preferred_element_type=F32)
        o_ref[:, c:c + MXU_COLS] = z.astype(BF16)


def _in_proj(xn, w_t, layer, *, tm, tn, first_row, n_blocks):
    t, d = xn.shape
    w_spec = pl.BlockSpec((pl.Element(1), pl.Element(tn), pl.Element(d)),
                          lambda j, i: (layer, pl.multiple_of(first_row + j * tn, BF16_SUBLANES), 0))
    return pl.pallas_call(
        _in_proj_kernel,
        grid=(n_blocks, t // tm),
        in_specs=[pl.BlockSpec((tm, d), lambda j, i: (i, 0)), w_spec],
        out_specs=pl.BlockSpec((tm, tn), lambda j, i: (i, j)),
        out_shape=jax.ShapeDtypeStruct((t, n_blocks * tn), BF16),
        scratch_shapes=[pltpu.VMEM((tn, d), BF16)],
        compiler_params=_params("arbitrary", "arbitrary"),
        name="in_proj",
    )(xn, w_t)


def _cast_specs(weights, layer, steps, step_index):
    ins, outs, shapes = [], [], []
    for w in weights:
        _, r, c = w.shape
        rb = r // steps
        assert rb * steps == r and rb % BF16_SUBLANES == 0
        ins.append(pl.BlockSpec((None, rb, c), lambda *g: (layer, step_index(*g), 0)))
        outs.append(pl.BlockSpec((rb, c), lambda *g: (step_index(*g), 0)))
        shapes.append(jax.ShapeDtypeStruct((r, c), BF16))
    return ins, outs, shapes


def _run_casts(src_refs, dst_refs):
    for src, dst in zip(src_refs, dst_refs):
        dst[...] = src[...].astype(BF16)


def _rows(c, tq):
    if isinstance(c, int):
        return slice(c * tq, (c + 1) * tq)
    return pl.ds(pl.multiple_of(c * tq, tq), tq)


def _pipelined_chunks(n, qk, sm, pv):
    def step(c, parity):
        qk(c, parity)
        sm(1 - parity)
        pv(c - 2, parity)

    qk(0, 0)
    qk(1, 1)
    sm(0)

    def body(i, carry):
        step(2 * i, 0)
        step(2 * i + 1, 1)
        return carry

    lax.fori_loop(1, n // 2, body, 0)
    sm(1)
    pv(n - 2, 0)
    pv(n - 1, 1)


def _fill_v_ones(va_ref, v_ref):
    w = v_ref.shape[1]
    va_ref[:, :w] = v_ref[...]
    va_ref[:, w:] = jnp.ones((va_ref.shape[0], va_ref.shape[1] - w), BF16)


def _exp_scores(s):
    return jnp.exp2(s - jnp.max(s, axis=-1, keepdims=True)).astype(BF16)


def _mla_attn_kernel(q_ref, k_ref, v_ref, *refs, tq, n_cast):
    o_ref = refs[n_cast]
    s0, s1, p0, p1, va_ref = refs[2 * n_cast + 1:]
    _run_casts(refs[:n_cast], refs[n_cast + 1:2 * n_cast + 1])
    sb, pb = (s0, s1), (p0, p1)
    dv = v_ref.shape[1]
    _fill_v_ones(va_ref, v_ref)

    def qk(c, slot):
        sb[slot][...] = lax.dot_general(q_ref[_rows(c, tq), :], k_ref[...], _NT, preferred_element_type=F32)

    def sm(slot):
        pb[slot][...] = _exp_scores(sb[slot][...])

    def pv(c, slot):
        o = jnp.dot(pb[slot][...], va_ref[...], preferred_element_type=F32)
        o_ref[_rows(c, tq), :] = (o[:, :dv] * (1.0 / o[:, dv:])).astype(BF16)

    _pipelined_chunks(q_ref.shape[0] // tq, qk, sm, pv)


def _mla_attn(q, k, v, cast_weights, layer, *, batch, tq):
    t = q.shape[0]
    s = t // batch
    c_in, c_out, c_shapes = _cast_specs(cast_weights, layer, batch * MLA_HEADS, lambda b, h: b * MLA_HEADS + h)
    return pl.pallas_call(
        functools.partial(_mla_attn_kernel, tq=tq, n_cast=len(cast_weights)),
        grid=(batch, MLA_HEADS),
        in_specs=[pl.BlockSpec((s, MLA_QK_PAD), lambda b, h: (b, h)),
                  pl.BlockSpec((s, MLA_QK_PAD), lambda b, h: (b, h)),
                  pl.BlockSpec((s, MLA_V_DIM), lambda b, h: (b, h))] + c_in,
        out_specs=[pl.BlockSpec((s, MLA_V_DIM), lambda b, h: (b, h))] + c_out,
        out_shape=[jax.ShapeDtypeStruct((t, MLA_HEADS * MLA_V_DIM), BF16)] + c_shapes,
        scratch_shapes=[pltpu.VMEM((tq, s), F32), pltpu.VMEM((tq, s), F32),
                        pltpu.VMEM((tq, s), BF16), pltpu.VMEM((tq, s), BF16),
                        pltpu.VMEM((s, 2 * MLA_V_DIM), BF16)],
        compiler_params=_params("parallel", "parallel"),
        name="mla_attn",
    )(q, k, v, *cast_weights)


def _diff_attn_kernel(lq1_ref, lk1_ref, lq2_ref, lk2_ref, gsub_ref, cos_ref, sin_ref, q_ref, k_ref, v_ref,
                      mem_ref, gm_ref, wckv_ref, *refs, tq, lam_init, q_scale, n_cast):
    o_ref, mkv_ref = refs[n_cast], refs[2 * n_cast + 1]
    s0, s1, p0, p1, va_ref, kr_ref, wckv_bf_ref = refs[2 * n_cast + 2:]
    _run_casts(refs[:n_cast], refs[n_cast + 1:2 * n_cast + 1])

    @pl.when((pl.program_id(0) == 0) & (pl.program_id(1) == 0))
    def _():
        wckv_bf_ref[...] = wckv_ref[...].astype(BF16)

    @pl.when(pl.program_id(1) == 0)
    def _():
        mn = _rms(mem_ref[...], gm_ref[...]).astype(BF16)
        mkv_ref[...] = jnp.dot(mn, wckv_bf_ref[...], preferred_element_type=F32).astype(BF16)

    lam = (jnp.exp(jnp.sum(lq1_ref[...] * lk1_ref[...], axis=-1, keepdims=True))
           - jnp.exp(jnp.sum(lq2_ref[...] * lk2_ref[...], axis=-1, keepdims=True)) + lam_init)
    gain = gsub_ref[...] * (1.0 - lam_init)
    lane = lax.broadcasted_iota(jnp.int32, (tq, 2 * DIFF_HEAD_DIM), 1)
    map0 = lane < DIFF_HEAD_DIM
    sb, pb = (s0, s1), (p0, p1)
    dv = v_ref.shape[1]
    _fill_v_ones(va_ref, v_ref)
    kr_ref[...] = _rope128(k_ref[...].astype(F32), cos_ref[...], sin_ref[...]).astype(BF16)

    def qk(c, slot):
        rows = _rows(c, tq)
        q = _rope128(q_ref[rows, :].astype(F32), cos_ref[rows, :], sin_ref[rows, :]) * q_scale
        q = q.astype(BF16)
        zero = jnp.zeros_like(q)
        q01 = jnp.concatenate([jnp.where(map0, q, zero), jnp.where(map0, zero, q)], axis=0)
        sb[slot][...] = lax.dot_general(q01, kr_ref[...], _NT, preferred_element_type=F32)

    def sm(slot):
        pb[slot][...] = _exp_scores(sb[slot][...])

    def pv(c, slot):
        o = jnp.dot(pb[slot][...], va_ref[...], preferred_element_type=F32)
        o = o[:, :dv] * (1.0 / o[:, dv:])
        o_ref[_rows(c, tq), :] = _rms(o[:tq] - lam * o[tq:], gain).astype(BF16)

    _pipelined_chunks(q_ref.shape[0] // tq, qk, sm, pv)


def _diff_attn(lq1, lk1, lq2, lk2, gsub, cos_t, sin_t, z, mem, gm, w_ckv, cast_weights, layer,
               *, batch, tq, q_col, k_col, v_col, lam_init):
    t = z.shape[0]
    s = t // batch
    hw = 2 * DIFF_HEAD_DIM
    m = mem.shape[0] // batch
    small = lambda a: pl.BlockSpec(a.shape, lambda b, h: (0, 0))
    head = lambda col: pl.BlockSpec((s, hw), lambda b, h: (b, col + h))
    c_in, c_out, c_shapes = _cast_specs(cast_weights, layer, batch * DIFF_HEADS, lambda b, h: b * DIFF_HEADS + h)
    return pl.pallas_call(
        functools.partial(_diff_attn_kernel, tq=tq, lam_init=lam_init, n_cast=len(cast_weights),
                          q_scale=LOG2_E * DIFF_HEAD_DIM ** -0.5),
        grid=(batch, DIFF_HEADS),
        in_specs=[small(lq1), small(lk1), small(lq2), small(lk2), small(gsub),
                  pl.BlockSpec((s, LANES), lambda b, h: (b, 0)), pl.BlockSpec((s, LANES), lambda b, h: (b, 0)),
                  head(q_col), head(k_col), head(v_col),
                  pl.BlockSpec((m, mem.shape[1]), lambda b, h: (b, 0)), small(gm),
                  pl.BlockSpec((None,) + w_ckv.shape[1:], lambda b, h: (layer, 0, 0),
                               pipeline_mode=pl.Buffered(1))] + c_in,
        out_specs=[pl.BlockSpec((s, DIFF_V_DIM), lambda b, h: (b, h))] + c_out
                  + [pl.BlockSpec((m, w_ckv.shape[2]), lambda b, h: (b, 0))],
        out_shape=[jax.ShapeDtypeStruct((t, DIFF_HEADS * DIFF_V_DIM), BF16)] + c_shapes
                  + [jax.ShapeDtypeStruct((mem.shape[0], w_ckv.shape[2]), BF16)],
        scratch_shapes=[pltpu.VMEM((2 * tq, s), F32), pltpu.VMEM((2 * tq, s), F32),
                        pltpu.VMEM((2 * tq, s), BF16), pltpu.VMEM((2 * tq, s), BF16),
                        pltpu.VMEM((s, 2 * DIFF_V_DIM), BF16), pltpu.VMEM((s, hw), BF16),
                        pltpu.VMEM(w_ckv.shape[1:], BF16)],
        compiler_params=_params("arbitrary", "arbitrary"),
        name="diff_attn",
    )(lq1, lk1, lq2, lk2, gsub, cos_t, sin_t, z, z, z, mem, gm, w_ckv, *cast_weights)


def _merge_cross_kernel(x_ref, oa_ref, ob_ref, ga0_ref, ga1_ref, gb0_ref, gb1_ref, woa_ref, wob_ref, wout_ref,
                        gc_ref, wcq_ref, mkv_ref, wco_ref, o_ref, *, q_scale):
    ya = jnp.dot(oa_ref[...], woa_ref[...], preferred_element_type=F32)
    yb = jnp.dot(ob_ref[...], wob_ref[...], preferred_element_type=F32)
    ga = _sigmoid(jnp.concatenate([ga0_ref[...], ga1_ref[...]], axis=1).astype(F32))
    gb = _sigmoid(jnp.concatenate([gb0_ref[...], gb1_ref[...]], axis=1).astype(F32))
    merged = ga * ya + gb * yb
    x1 = x_ref[...] + jnp.dot(merged.astype(BF16), wout_ref[...], preferred_element_type=F32)

    hn = _rms(x1, gc_ref[...]).astype(BF16)
    qc = (jnp.dot(hn, wcq_ref[...], preferred_element_type=F32) * q_scale).astype(BF16)
    hd = XATTN_HEAD_DIM
    heads = []
    for h in range(XATTN_HEADS):
        k_h = mkv_ref[:, 2 * h * hd:(2 * h + 1) * hd]
        v_h = mkv_ref[:, (2 * h + 1) * hd:(2 * h + 2) * hd]
        s = lax.dot_general(qc[:, h * hd:(h + 1) * hd], k_h, _NT, preferred_element_type=F32)
        e, l = _softmax_parts(s)
        o_h = jnp.dot(e.astype(BF16), v_h, preferred_element_type=F32) * (1.0 / l)
        heads.append(o_h.astype(BF16))
    oc = jnp.concatenate(heads, axis=1)
    o_ref[...] = x1 + jnp.dot(oc, wco_ref[...], preferred_element_type=F32)


def _merge_cross(x, oa, ob, z, gate_block, woa, wob, wout, gc, wcq, mkv, wco, *, tm, batch):
    t, d = x.shape
    blocks_per_seq = t // batch // tm
    m = mkv.shape[0] // batch
    const = lambda a: pl.BlockSpec(a.shape, lambda i: (0, 0), pipeline_mode=pl.Buffered(1))
    rows = lambda width: pl.BlockSpec((tm, width), lambda i: (i, 0))
    return pl.pallas_call(
        functools.partial(_merge_cross_kernel, q_scale=LOG2_E * XATTN_HEAD_DIM ** -0.5),
        grid=(t // tm,),
        in_specs=[rows(d), rows(oa.shape[1]), rows(ob.shape[1])]
                 + [pl.BlockSpec((tm, d // 2), lambda i, k=k: (i, gate_block + k)) for k in range(4)]
                 + [const(woa), const(wob), const(wout), const(gc), const(wcq),
                  pl.BlockSpec((m, mkv.shape[1]), lambda i: (i // blocks_per_seq, 0)),
                  const(wco)],
        out_specs=rows(d),
        out_shape=jax.ShapeDtypeStruct((t, d), F32),
        compiler_params=_params("parallel"),
        name="merge_cross",
    )(x, oa, ob, z, z, z, z, woa, wob, wout, gc, wcq, mkv, wco)


def _conv_ffn_kernel(xp_ref, x_ref, xn_ref, g_ref, wga_ref, wgb_ref, wva_ref, wvb_ref, c_ref, wda_ref, wdb_ref,
                     gfin_ref, o_ref, h_ref, uga_ref, uva_ref, ugb_ref, uvb_ref, acc_ref,
                     *, tm, halo, blocks_per_seq, n_chunks, final_norm):
    i, f = pl.program_id(0), pl.program_id(1)
    last = pl.num_programs(1) - 1

    @pl.when(f == 0)
    def _():
        g = g_ref[...]
        pos = i % blocks_per_seq
        keep_prev = jnp.where(pos == 0, 0.0, 1.0)
        keep_next = jnp.where(pos == blocks_per_seq - 1, 0.0, 1.0)
        h_ref[0:halo, :] = (_rms(xp_ref[...], g) * keep_prev).astype(BF16)
        h_ref[halo:halo + tm, :] = _rms(x_ref[...], g).astype(BF16)
        h_ref[halo + tm:, :] = (_rms(xn_ref[...], g) * keep_next).astype(BF16)
        acc_ref[...] = jnp.zeros_like(acc_ref)

    def conv(u_ref, c):
        return (u_ref[halo - 1:halo - 1 + tm, :] * c[0:1, :] + u_ref[halo:halo + tm, :] * c[1:2, :]
                + u_ref[halo + 1:halo + 1 + tm, :] * c[2:3, :] + c[3:4, :])

    def chunk(idx, wg_ref, wv_ref, wd_ref, ug_ref, uv_ref):
        h = h_ref[...]
        ug_ref[...] = jnp.dot(h, wg_ref[...], preferred_element_type=F32)
        uv_ref[...] = jnp.dot(h, wv_ref[...], preferred_element_type=F32)
        yg = conv(ug_ref, c_ref[idx])
        yv = conv(uv_ref, c_ref[n_chunks + idx])
        act = (yg * _sigmoid(yg) * yv).astype(BF16)
        return jnp.dot(act, wd_ref[...], preferred_element_type=F32)

    @pl.when(f < last)
    def _():
        acc_ref[...] += (chunk(2 * f, wga_ref, wva_ref, wda_ref, uga_ref, uva_ref)
                         + chunk(2 * f + 1, wgb_ref, wvb_ref, wdb_ref, ugb_ref, uvb_ref))

    @pl.when(f == last)
    def _():
        y = x_ref[...] + acc_ref[...] + chunk(2 * f, wga_ref, wva_ref, wda_ref, uga_ref, uva_ref)
        o_ref[...] = _rms(y, gfin_ref[...]) if final_norm else y


def _conv_ffn(x, g, w_up, conv_w, conv_b, w_down, g_final, *, tm, tf, batch, final_norm):
    t, d = x.shape
    ffn = w_down.shape[0]
    n_chunks = ffn // tf
    assert n_chunks % 2 == 1
    steps = n_chunks // 2 + 1
    halo = BF16_SUBLANES
    blocks_per_seq = t // batch // tm
    hb = tm // halo
    last_halo_block = t // halo - 1
    kern = functools.partial(_conv_ffn_kernel, tm=tm, halo=halo, blocks_per_seq=blocks_per_seq,
                             n_chunks=n_chunks, final_norm=final_norm)
    conv_wb = jnp.concatenate([conv_w, conv_b], axis=0)
    conv_wb = conv_wb.reshape(conv_wb.shape[0], 2 * n_chunks, tf).transpose(1, 0, 2)
    ca = lambda f: 2 * f
    cb = lambda f: jnp.minimum(2 * f + 1, n_chunks - 1)
    return pl.pallas_call(
        kern,
        grid=(t // tm, steps),
        in_specs=[pl.BlockSpec((halo, d), lambda i, f: (jnp.maximum(i * hb - 1, 0), 0)),
                  pl.BlockSpec((tm, d), lambda i, f: (i, 0)),
                  pl.BlockSpec((halo, d), lambda i, f: (jnp.minimum((i + 1) * hb, last_halo_block), 0)),
                  pl.BlockSpec((1, d), lambda i, f: (0, 0)),
                  pl.BlockSpec((d, tf), lambda i, f: (0, ca(f))),
                  pl.BlockSpec((d, tf), lambda i, f: (0, cb(f))),
                  pl.BlockSpec((d, tf), lambda i, f: (0, n_chunks + ca(f))),
                  pl.BlockSpec((d, tf), lambda i, f: (0, n_chunks + cb(f))),
                  pl.BlockSpec(conv_wb.shape, lambda i, f: (0, 0, 0)),
                  pl.BlockSpec((tf, d), lambda i, f: (ca(f), 0)),
                  pl.BlockSpec((tf, d), lambda i, f: (cb(f), 0)),
                  pl.BlockSpec((1, d), lambda i, f: (0, 0))],
        out_specs=pl.BlockSpec((tm, d), lambda i, f: (i, 0)),
        out_shape=jax.ShapeDtypeStruct((t, d), F32),
        scratch_shapes=[pltpu.VMEM((tm + 2 * halo, d), BF16)]
                       + [pltpu.VMEM((tm + 2 * halo, tf), F32)] * 4
                       + [pltpu.VMEM((tm, d), F32)],
        compiler_params=_params("parallel", "arbitrary"),
        name="conv_ffn",
    )(x, x, x, g, w_up, w_up, w_up, w_up, conv_wb, w_down, w_down, g_final)


def _rope_tables(positions):
    dim = MLA_ROPE_DIM
    inv = ROPE_THETA ** (-jnp.arange(0, dim, 2, dtype=F32) / dim)
    inv_t = jnp.tile(inv, LANES // (dim // 2))
    sign_t = jnp.tile(jnp.repeat(jnp.array([-1.0, 1.0], F32), dim // 2), LANES // dim)
    ang = positions.astype(F32).reshape(-1, 1) * inv_t
    return jnp.cos(ang), jnp.sin(ang) * sign_t


def kernel(x, mem, positions, g_mix_norm, w_in, g_q_norm, w_uq, g_kv_norm, w_ukv, w_o_mla, lambda_q1, lambda_k1, lambda_q2, lambda_k2, g_diff_sub, w_o_diff, w_out, g_cross_norm, g_mem_norm, w_cross_q, w_cross_kv, w_cross_o, g_ffn_norm, w_up, conv_w, conv_b, w_down, g_final):
    batch, seq, d = x.shape
    depth = w_in.shape[0]
    t = batch * seq
    assert depth >= 1 and MLA_ROPE_DIM == DIFF_HEAD_DIM and MLA_NOPE_DIM == MLA_V_DIM
    q_rank, kv_rank = g_q_norm.shape[1], g_kv_norm.shape[1]
    dqk = DIFF_HEADS * 2 * DIFF_HEAD_DIM
    dv_w = DIFF_HEADS * DIFF_V_DIM
    c0 = q_rank + kv_rank + MLA_ROPE_DIM
    tn = COLS_IN_PROJ
    assert dqk == tn and dv_w == tn and (2 * d) % tn == 0

    cos_t, sin_t = _rope_tables(positions)
    xf = x.reshape(t, d)
    memf = mem.reshape(-1, d)
    row = lambda v: v.reshape(1, -1)
    w_in_t = jnp.swapaxes(w_in, 1, 2)

    for l in range(depth):
        lam_init = 0.8 - 0.6 * math.exp(-0.3 * l)
        wq = jnp.pad(w_uq[l].reshape(q_rank, MLA_HEADS, MLA_NOPE_DIM + MLA_ROPE_DIM),
                     ((0, 0), (0, 0), (0, MLA_QK_PAD - MLA_NOPE_DIM - MLA_ROPE_DIM)))
        wq = wq.reshape(q_rank, MLA_HEADS * MLA_QK_PAD).astype(BF16)
        wkv = w_ukv[l].astype(BF16)

        q, k, v, xn = _mla_proj(xf, row(g_mix_norm[l]), w_in_t, l, c0 + LANES - MLA_ROPE_DIM,
                                row(g_q_norm[l]), row(g_kv_norm[l]), wq, wkv, cos_t, sin_t, tm=ROWS_MLA_PROJ)
        z = _in_proj(xn, w_in_t, l, tm=ROWS_IN_PROJ, tn=tn, first_row=c0, n_blocks=3 + 2 * d // tn)
        oa, w_up_b, w_down_b = _mla_attn(q, k, v, [w_up, w_down], l, batch=batch, tq=2 * Q_CHUNK)
        ob, w_oa_b, w_ob_b, w_out_b, w_cq_b, w_co_b, mkv = _diff_attn(
            row(lambda_q1[l]), row(lambda_k1[l]), row(lambda_q2[l]), row(lambda_k2[l]), row(g_diff_sub[l]),
            cos_t, sin_t, z, memf, row(g_mem_norm[l]), w_cross_kv,
            [w_o_mla, w_o_diff, w_out, w_cross_q, w_cross_o], l, batch=batch, tq=Q_CHUNK,
            q_col=0, k_col=DIFF_HEADS, v_col=2 * DIFF_HEADS, lam_init=lam_init)
        xf = _merge_cross(xf, oa, ob, z, 3 * tn // (d // 2), w_oa_b, w_ob_b, w_out_b, row(g_cross_norm[l]), w_cq_b, mkv,
                          w_co_b, tm=ROWS_MERGE, batch=batch)
        xf = _conv_ffn(xf, row(g_ffn_norm[l]), w_up_b, conv_w[l], row(conv_b[l]),
                       w_down_b, row(g_final), tm=ROWS_FFN, tf=COLS_FFN, batch=batch,
                       final_norm=(l == depth - 1))
    return xf.reshape(batch, seq, d)
```
